```python
import math
import jax, jax.numpy as jnp
from jax import lax
import numpy as np

D_MODEL = 2048
BATCH = 4
SEQ = 4096
DEPTH = 2

CTX_LEN = 256
GRID_W = 64
MIX_WIDTH = D_MODEL
F_GROUPS = 4
F_WIDTH = MIX_WIDTH // 4
F_GDIM = F_WIDTH // F_GROUPS
POOL_WINDOWS = (2, 4, 8, 16)
P_GROUPS = len(POOL_WINDOWS)
P_WIDTH = MIX_WIDTH // 4
P_GDIM = P_WIDTH // P_GROUPS
RET_WIDTH = MIX_WIDTH // 2
RET_HEADS = 8
RET_DK = RET_WIDTH // RET_HEADS
RET_CHUNK = 128
N_BRANCH = 3
ROPE_BASE = 10000.0
EPS = 1e-6

F_X_OFF = 0
F_G_OFF = F_X_OFF + F_WIDTH
P_X_OFF = F_G_OFF + F_WIDTH
P_G_OFF = P_X_OFF + P_WIDTH
R_Q_OFF = P_G_OFF + P_WIDTH
R_K_OFF = R_Q_OFF + RET_WIDTH
R_V_OFF = R_K_OFF + RET_WIDTH
R_G_OFF = R_V_OFF + RET_WIDTH
MG_OFF = R_G_OFF + RET_WIDTH
IN_WIDTH = MG_OFF + N_BRANCH * D_MODEL

kernel_name = "hybrid_fourier_pool_retention_dit"


def rms_norm(x, g):
    xf = x.astype(jnp.float32)
    y = xf * lax.rsqrt(jnp.mean(xf * xf, axis=-1, keepdims=True) + EPS)
    return (y * g.astype(jnp.float32)).astype(x.dtype)


def head_rms(o):
    return o * lax.rsqrt(jnp.mean(o * o, axis=-1, keepdims=True) + EPS)


def to_heads(u):
    B, N, _ = u.shape
    return u.reshape(B, N, RET_HEADS, RET_DK).transpose(0, 2, 1, 3).astype(jnp.float32)


def rope_axis(x, pos):
    nf = x.shape[-1] // 2
    inv = ROPE_BASE ** (-jnp.arange(nf, dtype=jnp.float32) / nf)
    ang = pos.astype(jnp.float32)[:, None] * inv[None, :]
    cos, sin = jnp.cos(ang), jnp.sin(ang)
    x1, x2 = x[..., :nf], x[..., nf:]
    return jnp.concatenate([x1 * cos - x2 * sin, x1 * sin + x2 * cos], axis=-1)


def rope_2d(x, rows, cols):
    h = x.shape[-1] // 2
    return jnp.concatenate([rope_axis(x[..., :h], rows), rope_axis(x[..., h:], cols)], axis=-1)


def fourier_mix(u, w_fourier):
    B, N, _ = u.shape
    ug = u.astype(jnp.float32).reshape(B, N, F_GROUPS, F_GDIM)
    mixed = jnp.fft.fft2(ug, axes=(1, 3), norm="ortho").real
    y = jnp.einsum("bngc,gcd->bngd", mixed, w_fourier.astype(jnp.float32))
    return y.reshape(B, N, F_WIDTH).astype(u.dtype)


def pool_mix(u, w_pool, pool_scale):
    B, N, _ = u.shape
    uf = u.astype(jnp.float32).reshape(B, N, P_GROUPS, P_GDIM)
    cs = jnp.concatenate([jnp.zeros((B, 1, P_GROUPS, P_GDIM), jnp.float32), jnp.cumsum(uf, axis=1)], axis=1)
    t = jnp.arange(N)
    outs = []
    for g, w in enumerate(POOL_WINDOWS):
        lo = jnp.clip(t - w // 2, 0, N)
        hi = jnp.clip(t + w // 2, 0, N)
        cs_g = cs[:, :, g]
        cnt = (hi - lo).astype(jnp.float32)[None, :, None]
        outs.append((cs_g[:, hi] - cs_g[:, lo]) / cnt - uf[:, :, g])
    pooled = jnp.stack(outs, axis=2)
    y = jnp.einsum("bngc,gcd->bngd", pooled, w_pool.astype(jnp.float32)).reshape(B, N, P_WIDTH)
    return (y * pool_scale.astype(jnp.float32)).astype(u.dtype)


def retention_scan(q, k, v, log_gamma, s0):
    B, H, N, d = q.shape
    C = RET_CHUNK
    nc = N // C
    qc = q.reshape(B, H, nc, C, d)
    kc = k.reshape(B, H, nc, C, d)
    vc = v.reshape(B, H, nc, C, d)
    i = jnp.arange(C, dtype=jnp.float32)
    lg = log_gamma[:, None]
    diff = i[:, None] - i[None, :]
    mask = jnp.where(diff >= 0, jnp.exp(lg[:, :, None] * jnp.maximum(diff, 0.0)), 0.0)
    scores = jnp.einsum("bhnid,bhnjd->bhnij", qc, kc) * mask[None, :, None]
    o_intra = jnp.einsum("bhnij,bhnje->bhnie", scores, vc)
    k_dec = kc * jnp.exp(lg * (C - 1 - i))[None, :, None, :, None]
    kv = jnp.einsum("bhnjd,bhnje->nbhde", k_dec, vc)
    chunk_decay = jnp.exp(log_gamma * C)[None, :, None, None]

    def step(s, kv_n):
        return chunk_decay * s + kv_n, s

    s_final, s_prev = lax.scan(step, s0, kv)
    q_dec = qc * jnp.exp(lg * (i + 1))[None, :, None, :, None]
    o_cross = jnp.einsum("bhnid,nbhde->bhnie", q_dec, s_prev)
    return (o_intra + o_cross).reshape(B, H, N, d), s_final


def retention_final_state(k, v, log_gamma):
    N = k.shape[2]
    w = jnp.exp(log_gamma[:, None] * (N - 1 - jnp.arange(N, dtype=jnp.float32))[None, :])
    return jnp.einsum("bhnd,hn,bhne->bhde", k, w, v)


def hybrid_mixer(h, w_in, w_fourier, w_pool, pool_scale, log_gamma,
                 w_up_fourier, w_up_pool, w_up_ret, w_out, pos, s0_fwd, s0_bwd):
    B, N, _ = h.shape
    proj = h @ w_in
    f_x, f_g = proj[..., F_X_OFF:F_G_OFF], proj[..., F_G_OFF:P_X_OFF]
    p_x, p_g = proj[..., P_X_OFF:P_G_OFF], proj[..., P_G_OFF:R_Q_OFF]
    q = to_heads(proj[..., R_Q_OFF:R_K_OFF]) * (RET_DK ** -0.5)
    k = to_heads(proj[..., R_K_OFF:R_V_OFF])
    v = to_heads(proj[..., R_V_OFF:R_G_OFF])
    r_g = proj[..., R_G_OFF:MG_OFF]
    gate_logits = proj[..., MG_OFF:].reshape(B, N, N_BRANCH, D_MODEL)
    if pos is not None:
        q = rope_2d(q, pos[0], pos[1])
        k = rope_2d(k, pos[0], pos[1])
    y_f = (fourier_mix(f_x, w_fourier) * jax.nn.silu(f_g)) @ w_up_fourier
    y_p = (pool_mix(p_x, w_pool, pool_scale) * jax.nn.silu(p_g)) @ w_up_pool
    o_f, s_f = retention_scan(q, k, v, log_gamma[0], s0_fwd)
    o_b, s_b = retention_scan(jnp.flip(q, 2), jnp.flip(k, 2), jnp.flip(v, 2), log_gamma[1], s0_bwd)
    o = head_rms(o_f + jnp.flip(o_b, 2)).transpose(0, 2, 1, 3).reshape(B, N, RET_WIDTH).astype(h.dtype)
    y_r = (o * jax.nn.silu(r_g)) @ w_up_ret
    g = jax.nn.sigmoid(gate_logits.astype(jnp.float32)).astype(h.dtype)
    merged = g[:, :, 0] * y_f + g[:, :, 1] * y_p + g[:, :, 2] * y_r
    return (merged @ w_out).astype(h.dtype), s_f, s_b


def setup_inputs(seed: int = 0) -> dict:
    key = jax.random.key(seed)
    ks = jax.random.split(key, 20)
    f32 = jnp.float32
    D = D_MODEL

    def nrm(k, shape, scale):
        return jax.random.normal(k, shape, f32) * scale

    base_decay = (5.0 + jnp.arange(RET_HEADS, dtype=f32)) * math.log(2.0)
    return {
        "x": nrm(ks[0], (BATCH, SEQ, D), 1.0),
        "c": nrm(ks[1], (BATCH, D), 1.0),
        "ctx": nrm(ks[2], (BATCH, CTX_LEN, D), 1.0),
        "c_ctx": nrm(ks[3], (D,), 1.0),
        "w_ada": nrm(ks[4], (DEPTH, D, 3 * D), D ** -0.5),
        "b_ada": nrm(ks[5], (DEPTH, 3 * D), 0.01),
        "norm_g": 1.0 + nrm(ks[6], (DEPTH, D), 0.02),
        "w_in": nrm(ks[7], (DEPTH, D, IN_WIDTH), D ** -0.5),
        "w_fourier": nrm(ks[8], (DEPTH, F_GROUPS, F_GDIM, F_GDIM), F_GDIM ** -0.5),
        "w_pool": nrm(ks[9], (DEPTH, P_GROUPS, P_GDIM, P_GDIM), P_GDIM ** -0.5),
        "pool_scale": 1.0 + nrm(ks[10], (DEPTH, P_WIDTH), 0.02),
        "ret_decay_logit": base_decay[None, None, :] + nrm(ks[11], (DEPTH, 2, RET_HEADS), 0.1),
        "w_up_fourier": nrm(ks[12], (DEPTH, F_WIDTH, D), F_WIDTH ** -0.5),
        "w_up_pool": nrm(ks[13], (DEPTH, P_WIDTH, D), P_WIDTH ** -0.5),
        "w_up_ret": nrm(ks[14], (DEPTH, RET_WIDTH, D), RET_WIDTH ** -0.5),
        "w_out": nrm(ks[15], (DEPTH, D, D), D ** -0.5),
        "final_norm_g": 1.0 + nrm(ks[16], (D,), 0.02),
    }


def reference(x, c, ctx, c_ctx, w_ada, b_ada, norm_g, w_in, w_fourier, w_pool, pool_scale,
              ret_decay_logit, w_up_fourier, w_up_pool, w_up_ret, w_out, final_norm_g):
    B, N, _ = x.shape
    ROWS = N // GRID_W
    grid_r, grid_c = jnp.meshgrid(jnp.arange(ROWS), jnp.arange(GRID_W), indexing="ij")
    pos = (grid_r.reshape(-1), grid_c.reshape(-1))
    silu_c = jax.nn.silu(c)
    silu_cc = jax.nn.silu(c_ctx)
    s_zero = jnp.zeros((ctx.shape[0], RET_HEADS, RET_DK, RET_DK), jnp.float32)
    for l in range(DEPTH):
        last = l == DEPTH - 1
        mod = silu_c @ w_ada[l] + b_ada[l]
        shift, scale, gate = jnp.split(mod[:, None, :], 3, axis=-1)
        mod_c = silu_cc @ w_ada[l] + b_ada[l]
        shift_c, scale_c, gate_c = jnp.split(mod_c, 3, axis=-1)
        log_gamma = jax.nn.log_sigmoid(ret_decay_logit[l].astype(jnp.float32))
        h = rms_norm(x, norm_g[l]) * (1.0 + scale) + shift
        hc = rms_norm(ctx, norm_g[l]) * (1.0 + scale_c) + shift_c
        if last:
            kv_c = hc @ w_in[l][:, R_K_OFF:R_G_OFF]
            k_c = to_heads(kv_c[..., :RET_WIDTH])
            v_c = to_heads(kv_c[..., RET_WIDTH:])
            s_f = retention_final_state(k_c, v_c, log_gamma[0])
            s_b = retention_final_state(jnp.flip(k_c, 2), jnp.flip(v_c, 2), log_gamma[1])
        else:
            y_c, s_f, s_b = hybrid_mixer(hc, w_in[l], w_fourier[l], w_pool[l], pool_scale[l], log_gamma,
                                         w_up_fourier[l], w_up_pool[l], w_up_ret[l], w_out[l],
                                         None, s_zero, s_zero)
        y, _, _ = hybrid_mixer(h, w_in[l], w_fourier[l], w_pool[l], pool_scale[l], log_gamma,
                               w_up_fourier[l], w_up_pool[l], w_up_ret[l], w_out[l],
                               pos, s_f, s_b)
        x = x + gate * y
        if not last:
            ctx = ctx + gate_c * y_c
    return rms_norm(x, final_norm_g)
```

```python
import functools
import math

import numpy as np
import jax
import jax.numpy as jnp
from jax import lax
from jax.experimental import pallas as pl
from jax.experimental.pallas import tpu as pltpu

F32 = jnp.float32
BF16 = jnp.bfloat16

D_MODEL = 2048
GRID_W = 64
F_GROUPS = 4
F_WIDTH = D_MODEL // 4
F_GDIM = F_WIDTH // F_GROUPS
POOL_WINDOWS = (2, 4, 8, 16)
P_WIDTH = D_MODEL // 4
P_GDIM = P_WIDTH // len(POOL_WINDOWS)
RET_WIDTH = D_MODEL // 2
RET_HEADS = 8
RET_DK = RET_WIDTH // RET_HEADS
N_BRANCH = 3
ROPE_BASE = 10000.0
EPS = 1e-6

F_X_OFF = 0
F_G_OFF = F_X_OFF + F_WIDTH
P_X_OFF = F_G_OFF + F_WIDTH
P_G_OFF = P_X_OFF + P_WIDTH
R_Q_OFF = P_G_OFF + P_WIDTH
R_K_OFF = R_Q_OFF + RET_WIDTH
R_V_OFF = R_K_OFF + RET_WIDTH
R_G_OFF = R_V_OFF + RET_WIDTH
MG_OFF = R_G_OFF + RET_WIDTH
IN_WIDTH = MG_OFF + N_BRANCH * D_MODEL

V7X_VMEM_BYTES = 64 * 1024 * 1024
VMEM_CAP_BYTES = V7X_VMEM_BYTES - 8 * 1024 * 1024
MOD_ROWS = 8
POOL_HALO = 16
RET_CHUNK = 256
IN_TM, IN_TN = 1024, 1024
MERGE_TM = 512
DFT_TM = 256
POOL_TM = 256


def _vmem_limit(estimate_bytes):
    return int(min(max(estimate_bytes * 5 // 4 + (4 << 20), 32 << 20), VMEM_CAP_BYTES))


def _params(sem, estimate_bytes):
    return pltpu.CompilerParams(dimension_semantics=sem, vmem_limit_bytes=_vmem_limit(estimate_bytes))


def _const_spec(shape):
    nd = len(shape)
    return pl.BlockSpec(shape, lambda *_: (0,) * nd, pipeline_mode=pl.Buffered(1))


def _silu(v):
    return v * jax.nn.sigmoid(v)


def _dot(a, b):
    return jnp.dot(a, b, preferred_element_type=F32)


def _ada_kernel(s_ref, w_ref, b_ref, o_ref):
    s = _silu(s_ref[...])
    o_ref[0] = _dot(s.astype(BF16), w_ref[0].astype(BF16)) + b_ref[0]


def _ada_call(cond_rows, w_ada, b_ada):
    depth, d, w3 = w_ada.shape
    tn = 1024
    return pl.pallas_call(
        _ada_kernel,
        grid=(depth, w3 // tn),
        in_specs=[
            pl.BlockSpec((MOD_ROWS, d), lambda l, j: (0, 0)),
            pl.BlockSpec((1, d, tn), lambda l, j: (l, 0, j)),
            pl.BlockSpec((1, 1, tn), lambda l, j: (l, 0, j)),
        ],
        out_specs=pl.BlockSpec((1, MOD_ROWS, tn), lambda l, j: (l, 0, j)),
        out_shape=jax.ShapeDtypeStruct((depth, MOD_ROWS, w3), F32),
        compiler_params=_params(("arbitrary", "arbitrary"), 2 * d * tn * 4 + d * tn * 2),
        name="ada_mod",
    )(cond_rows, w_ada, b_ada.reshape(depth, 1, w3))


def _inproj_kernel(x_ref, g_ref, sc_ref, sh_ref, w_ref, o_ref, h_ref):
    @pl.when(pl.program_id(2) == 0)
    def _():
        x = x_ref[0]
        ms = jnp.mean(x * x, axis=-1, keepdims=True)
        y = x * lax.rsqrt(ms + EPS) * g_ref[...]
        h_ref[...] = (y * (1.0 + sc_ref[0]) + sh_ref[0]).astype(BF16)

    o_ref[0] = _dot(h_ref[...], w_ref[...]).astype(o_ref.dtype)


def _inproj_call(x, norm_g, mod3, mod_row, w_bf16, col0, ncols):
    bsz, n, d = x.shape
    tm = min(IN_TM, n)
    tn = IN_TN
    cb0 = col0 // tn
    row = (lambda b: b) if mod_row is None else (lambda b: mod_row)
    est = 2 * tm * d * 4 + 2 * d * tn * 2 + 2 * tm * tn * 2 + tm * d * 2 + tm * tn * 4 + tm * d * 4
    return pl.pallas_call(
        _inproj_kernel,
        grid=(bsz, n // tm, ncols // tn),
        in_specs=[
            pl.BlockSpec((1, tm, d), lambda b, i, j: (b, i, 0)),
            pl.BlockSpec((1, d), lambda b, i, j: (0, 0)),
            pl.BlockSpec((1, 1, d), lambda b, i, j: (row(b), 0, 1)),
            pl.BlockSpec((1, 1, d), lambda b, i, j: (row(b), 0, 0)),
            pl.BlockSpec((d, tn), lambda b, i, j: (0, cb0 + j)),
        ],
        out_specs=pl.BlockSpec((1, tm, tn), lambda b, i, j: (b, i, j)),
        out_shape=jax.ShapeDtypeStruct((bsz, n, ncols), BF16),
        scratch_shapes=[pltpu.VMEM((tm, d), BF16)],
        compiler_params=_params(("arbitrary", "arbitrary", "arbitrary"), est),
        name="in_proj",
    )(x, norm_g.reshape(1, d), mod3, mod3, w_bf16)


@functools.lru_cache(maxsize=None)
def _dft_tables(n, tm):
    k = np.arange(n, dtype=np.int64)[None, :]
    r = np.arange(tm, dtype=np.int64)[:, None]
    ang_b = 2.0 * np.pi * ((r * k) % n) / n
    n0 = (np.arange(n // tm, dtype=np.int64) * tm)[:, None]
    ang_a = 2.0 * np.pi * ((n0 * k) % n) / n
    c = np.arange(F_GDIM, dtype=np.int64)
    ang_c = 2.0 * np.pi * ((c[:, None] * c[None, :]) % F_GDIM) / F_GDIM
    ortho = 1.0 / math.sqrt(n * F_GDIM)
    f = lambda a: np.asarray(a, dtype=np.float32)
    return (f(np.cos(ang_b)), f(np.sin(ang_b)),
            f(np.cos(ang_a))[:, None, :], f(np.sin(ang_a))[:, None, :],
            f(np.cos(ang_c) * ortho), f(np.sin(ang_c) * ortho))


def _fourier_kernel(*refs, bsz, tm):
    cb_ref, sb_ref, ca_ref, sa_ref, cc_ref, sc_ref, wf_ref = refs[:7]
    x_refs = refs[7:7 + bsz]
    fg_ref, o_ref, lhs_ref, ab_ref = refs[7 + bsz:]
    gd = F_GDIM

    @pl.when(pl.program_id(0) == 0)
    def _():
        for g in range(F_GROUPS):
            w = wf_ref[g]
            ab_ref[g, :gd, :] = jnp.dot(cc_ref[...], w, preferred_element_type=F32,
                                        precision=lax.Precision.HIGHEST).astype(BF16)
            ab_ref[g, gd:, :] = (-jnp.dot(sc_ref[...], w, preferred_element_type=F32,
                                          precision=lax.Precision.HIGHEST)).astype(BF16)

    ca, sa = ca_ref[0], sa_ref[0]
    cb, sb = cb_ref[...], sb_ref[...]
    lhs_ref[:tm, :] = (ca * cb - sa * sb).astype(BF16)
    lhs_ref[tm:, :] = (sa * cb + ca * sb).astype(BF16)

    for b in range(bsz):
        p = _dot(lhs_ref[...], x_refs[b][0])
        for g in range(F_GROUPS):
            sl = slice(g * gd, (g + 1) * gd)
            cat = jnp.concatenate([p[:tm, sl], p[tm:, sl]], axis=1).astype(BF16)
            y = _dot(cat, ab_ref[g])
            o_ref[b, :, sl] = (y * _silu(fg_ref[b, :, sl].astype(F32))).astype(o_ref.dtype)


def _fourier_call(proj, w_fourier):
    bsz, n, _ = proj.shape
    tm = min(DFT_TM, n)
    cb, sb, ca, sa, cc, sc = _dft_tables(n, tm)
    kern = functools.partial(_fourier_kernel, bsz=bsz, tm=tm)
    x_specs = [pl.BlockSpec((1, n, F_WIDTH), functools.partial(lambda i, b: (b, 0, F_X_OFF // F_WIDTH), b=b),
                            pipeline_mode=pl.Buffered(1)) for b in range(bsz)]
    est = (2 * tm * n * 4 + bsz * n * F_WIDTH * 2 + 2 * tm * n * 2 + 4 * bsz * tm * F_WIDTH * 2
           + 2 * tm * F_WIDTH * 4 + 3 * tm * n * 4)
    return pl.pallas_call(
        kern,
        grid=(n // tm,),
        in_specs=[
            _const_spec((tm, n)), _const_spec((tm, n)),
            pl.BlockSpec((1, 1, n), lambda i: (i, 0, 0)),
            pl.BlockSpec((1, 1, n), lambda i: (i, 0, 0)),
            _const_spec((F_GDIM, F_GDIM)), _const_spec((F_GDIM, F_GDIM)),
            _const_spec((F_GROUPS, F_GDIM, F_GDIM)),
            *x_specs,
            pl.BlockSpec((bsz, tm, F_WIDTH), lambda i: (0, i, F_G_OFF // F_WIDTH)),
        ],
        out_specs=pl.BlockSpec((bsz, tm, F_WIDTH), lambda i: (0, i, 0)),
        out_shape=jax.ShapeDtypeStruct((bsz, n, F_WIDTH), BF16),
        scratch_shapes=[pltpu.VMEM((2 * tm, n), BF16),
                        pltpu.VMEM((F_GROUPS, 2 * F_GDIM, F_GDIM), BF16)],
        compiler_params=_params(("arbitrary",), est),
        name="fourier",
    )(cb, sb, ca, sa, cc, sc, w_fourier, *([proj] * bsz), proj)


@functools.lru_cache(maxsize=None)
def _pool_bands(tm):
    r = np.arange(tm)[:, None]
    main, prev, nxt = [], [], []
    for w in POOL_WINDOWS:
        hw = w // 2
        band = lambda d: ((d >= -hw) & (d <= hw - 1)).astype(np.float32)
        main.append(band(np.arange(tm)[None, :] - r))
        prev.append(band(np.arange(POOL_HALO)[None, :] - POOL_HALO - r))
        nxt.append(band(np.arange(POOL_HALO)[None, :] + tm - r))
    return np.stack(main), np.stack(prev), np.stack(nxt)


def _pool_kernel(xm_ref, xp_ref, xn_ref, pg_ref, bm_ref, bp_ref, bn_ref, w_ref, ps_ref, o_ref, *, n, tm):
    i = pl.program_id(1)
    has_prev = jnp.where(i > 0, 1.0, 0.0).astype(F32)
    has_next = jnp.where(i < pl.num_programs(1) - 1, 1.0, 0.0).astype(F32)
    t = i * tm + lax.broadcasted_iota(jnp.int32, (tm, P_GDIM), 0)
    for g, w in enumerate(POOL_WINDOWS):
        sl = slice(g * P_GDIM, (g + 1) * P_GDIM)
        xg = xm_ref[0, :, sl]
        s = _dot(bm_ref[g], xg)
        s = s + has_prev * _dot(bp_ref[g], xp_ref[0, :, sl])
        s = s + has_next * _dot(bn_ref[g], xn_ref[0, :, sl])
        cnt = (jnp.minimum(t + w // 2, n) - jnp.maximum(t - w // 2, 0)).astype(F32)
        pooled = s / cnt - xg.astype(F32)
        y = _dot(pooled.astype(BF16), w_ref[g].astype(BF16)) * ps_ref[:, sl]
        o_ref[0, :, sl] = (y * _silu(pg_ref[0, :, sl].astype(F32))).astype(o_ref.dtype)


def _pool_call(proj, w_pool, pool_scale):
    bsz, n, _ = proj.shape
    tm = min(POOL_TM, n)
    bands = [jnp.asarray(a, dtype=BF16) for a in _pool_bands(tm)]
    hb = tm // POOL_HALO
    nhb = n // POOL_HALO
    xcol = P_X_OFF // P_WIDTH
    kern = functools.partial(_pool_kernel, n=n, tm=tm)
    return pl.pallas_call(
        kern,
        grid=(bsz, n // tm),
        in_specs=[
            pl.BlockSpec((1, tm, P_WIDTH), lambda b, i: (b, i, xcol)),
            pl.BlockSpec((1, POOL_HALO, P_WIDTH), lambda b, i: (b, jnp.maximum(i * hb - 1, 0), xcol)),
            pl.BlockSpec((1, POOL_HALO, P_WIDTH), lambda b, i: (b, jnp.minimum((i + 1) * hb, nhb - 1), xcol)),
            pl.BlockSpec((1, tm, P_WIDTH), lambda b, i: (b, i, P_G_OFF // P_WIDTH)),
            _const_spec(bands[0].shape), _const_spec(bands[1].shape), _const_spec(bands[2].shape),
            _const_spec(w_pool.shape),
            _const_spec((1, P_WIDTH)),
        ],
        out_specs=pl.BlockSpec((1, tm, P_WIDTH), lambda b, i: (b, i, 0)),
        out_shape=jax.ShapeDtypeStruct((bsz, n, P_WIDTH), BF16),
        compiler_params=_params(("arbitrary", "arbitrary"), 16 << 20),
        name="pool",
    )(proj, proj, proj, proj, *bands, w_pool, pool_scale.reshape(1, P_WIDTH))


@functools.lru_cache(maxsize=None)
def _rope_tables(n):
    pos = np.arange(n)
    nf = RET_DK // 4
    inv = ROPE_BASE ** (-np.arange(nf, dtype=np.float64) / nf)
    ang_r = (pos // GRID_W)[:, None] * inv[None, :]
    ang_c = (pos % GRID_W)[:, None] * inv[None, :]
    cos = np.concatenate([np.cos(ang_r)] * 2 + [np.cos(ang_c)] * 2, axis=1)
    sin = np.concatenate([-np.sin(ang_r), np.sin(ang_r), -np.sin(ang_c), np.sin(ang_c)], axis=1)
    return np.asarray(cos, np.float32), np.asarray(sin, np.float32)


def _log_sigmoid(v):
    return jnp.minimum(v, 0.0) - jnp.log1p(jnp.exp(-jnp.abs(v)))


def _retention_kernel(*refs, n, rope, want_out, want_states):
    it = iter(refs)
    dl_ref = next(it)
    q_ref, k_ref, v_ref = next(it), next(it), next(it)
    rg_ref = next(it) if want_out else None
    cos_ref, sin_ref = (next(it), next(it)) if rope else (None, None)
    s0f_ref, s0b_ref = next(it), next(it)
    o_ref = next(it) if want_out else None
    sfo_ref, sbo_ref = (next(it), next(it)) if want_states else (None, None)
    q_s, k_s, sb_s, sf_s, m_s, d_s = it

    ch, dk = RET_CHUNK, RET_DK
    nc = n // ch
    h = pl.program_id(1)

    def log_gamma(direction, shape):
        return _log_sigmoid(jnp.full(shape, dl_ref[direction, h], F32))

    lg_f, lg_b = log_gamma(0, (ch, dk)), log_gamma(1, (ch, dk))
    row = lax.broadcasted_iota(jnp.int32, (ch, dk), 0).astype(F32)
    d_s[0] = jnp.exp(lg_f * (row + 1.0))
    d_s[1] = jnp.exp(lg_b * (ch - row))
    d_s[2] = jnp.exp(lg_f * (ch - 1.0 - row))
    d_s[3] = jnp.exp(lg_b * row)
    gf_c = jnp.exp(log_gamma(0, (dk, dk)) * ch)
    gb_c = jnp.exp(log_gamma(1, (dk, dk)) * ch)
    if want_out:
        diff = (lax.broadcasted_iota(jnp.int32, (ch, ch), 0)
                - lax.broadcasted_iota(jnp.int32, (ch, ch), 1)).astype(F32)
        lf, lb = log_gamma(0, (ch, ch)), log_gamma(1, (ch, ch))
        m_s[...] = (jnp.where(diff >= 0, jnp.exp(lf * jnp.maximum(diff, 0.0)), 0.0)
                    + jnp.where(diff <= 0, jnp.exp(lb * jnp.maximum(-diff, 0.0)), 0.0))

    def chunk(c):
        return pl.ds(pl.multiple_of(c * ch, ch), ch)

    def rotate(u, cos, sin):
        lane = lax.broadcasted_iota(jnp.int32, u.shape, 1)
        partner = jnp.where((lane % (dk // 2)) < dk // 4,
                            pltpu.roll(u, dk - dk // 4, 1), pltpu.roll(u, dk // 4, 1))
        return u * cos + partner * sin

    def prep(c, carry):
        sl = chunk(c)
        q = q_ref[0, sl, :].astype(F32) * (dk ** -0.5)
        k = k_ref[0, sl, :].astype(F32)
        if rope:
            cos, sin = cos_ref[sl, :], sin_ref[sl, :]
            q, k = rotate(q, cos, sin), rotate(k, cos, sin)
        q_s[sl, :] = q
        k_s[sl, :] = k
        return carry

    lax.fori_loop(0, nc, prep, 0)

    def kv_state(sl, decay):
        kd = (k_s[sl, :] * decay).astype(BF16)
        return lax.dot_general(kd, v_ref[0, sl, :], (((0,), (0,)), ((), ())), preferred_element_type=F32)

    def bwd(t, s):
        c = nc - 1 - t
        sb_s[c] = s
        return gb_c * s + kv_state(chunk(c), d_s[3])

    s_b = lax.fori_loop(0, nc, bwd, s0b_ref[0, 0])
    if want_states:
        sbo_ref[0, 0] = s_b

    sf_s[...] = s0f_ref[0, 0]

    def fwd(c, carry):
        sl = chunk(c)
        if want_out:
            q = q_s[sl, :]
            s = lax.dot_general(q.astype(BF16), k_s[sl, :].astype(BF16), (((1,), (1,)), ((), ())),
                                preferred_element_type=F32)
            o = _dot((s * m_s[...]).astype(BF16), v_ref[0, sl, :])
            qd = jnp.concatenate([q * d_s[0], q * d_s[1]], axis=1).astype(BF16)
            st = jnp.concatenate([sf_s[...], sb_s[c]], axis=0).astype(BF16)
            o = o + _dot(qd, st)
            o = o * lax.rsqrt(jnp.mean(o * o, axis=-1, keepdims=True) + EPS)
            o_ref[0, sl, :] = (o * _silu(rg_ref[0, sl, :].astype(F32))).astype(o_ref.dtype)
        sf_s[...] = gf_c * sf_s[...] + kv_state(sl, d_s[2])
        return carry

    lax.fori_loop(0, nc, fwd, 0)
    if want_states:
        sfo_ref[0, 0] = sf_s[...]


def _retention_call(proj, q_off, decay_logit, s0_f, s0_b, *, rope, want_out, want_states):
    bsz, n, _ = proj.shape
    nh, dk, ch = RET_HEADS, RET_DK, RET_CHUNK
    assert n % ch == 0
    cb = q_off // dk
    col = lambda j: pl.BlockSpec((1, n, dk), functools.partial(lambda b, h, j: (b, 0, cb + j * nh + h), j=j))
    st_spec = pl.BlockSpec((1, 1, dk, dk), lambda b, h: (b, h, 0, 0))
    in_specs = [pl.BlockSpec(memory_space=pltpu.SMEM), col(0), col(1), col(2)]
    args = [decay_logit, proj, proj, proj]
    if want_out:
        in_specs.append(col(3))
        args.append(proj)
    if rope:
        cos, sin = _rope_tables(n)
        in_specs += [_const_spec((n, dk)), _const_spec((n, dk))]
        args += [cos, sin]
    in_specs += [st_spec, st_spec]
    args += [s0_f, s0_b]
    out_specs, out_shape = [], []
    if want_out:
        out_specs.append(pl.BlockSpec((1, n, dk), lambda b, h: (b, 0, h)))
        out_shape.append(jax.ShapeDtypeStruct((bsz, n, RET_WIDTH), BF16))
    if want_states:
        out_specs += [st_spec, st_spec]
        out_shape += [jax.ShapeDtypeStruct((bsz, nh, dk, dk), F32)] * 2
    kern = functools.partial(_retention_kernel, n=n, rope=rope, want_out=want_out, want_states=want_states)
    est = (10 * n * dk * 2 + 2 * n * dk * 4 + 2 * n * dk * 4 + (n // ch) * dk * dk * 4
           + ch * ch * 4 + 4 * ch * dk * 4 + 8 * ch * ch * 4)
    return pl.pallas_call(
        kern,
        grid=(bsz, nh),
        in_specs=in_specs,
        out_specs=out_specs,
        out_shape=out_shape,
        scratch_shapes=[pltpu.VMEM((n, dk), F32), pltpu.VMEM((n, dk), F32),
                        pltpu.VMEM((n // ch, dk, dk), F32), pltpu.VMEM((dk, dk), F32),
                        pltpu.VMEM((ch, ch), F32), pltpu.VMEM((4, ch, dk), F32)],
        compiler_params=_params(("arbitrary", "arbitrary"), est),
        name="retention",
    )(*args)


def _merge_kernel(uf_ref, up_ref, ur_ref, g0_ref, g1_ref, g2_ref, x_ref, gate_ref,
                  wf_ref, wp_ref, wr_ref, wo_ref, fg_ref, o_ref, *, final):
    m = jax.nn.sigmoid(g0_ref[0].astype(F32)) * _dot(uf_ref[0], wf_ref[...])
    m = m + jax.nn.sigmoid(g1_ref[0].astype(F32)) * _dot(up_ref[0], wp_ref[...])
    m = m + jax.nn.sigmoid(g2_ref[0].astype(F32)) * _dot(ur_ref[0], wr_ref[...])
    y = _dot(m.astype(BF16), wo_ref[...])
    xn = x_ref[0] + gate_ref[0] * y
    if final:
        ms = jnp.mean(xn * xn, axis=-1, keepdims=True)
        xn = xn * lax.rsqrt(ms + EPS) * fg_ref[...]
    o_ref[0] = xn


def _merge_call(u_f, u_p, u_r, proj, x, mod3, mod_row, w_up_f, w_up_p, w_up_r, w_out, final_g, final):
    bsz, n, d = x.shape
    tm = min(MERGE_TM, n)
    row = (lambda b: b) if mod_row is None else (lambda b: mod_row)
    tok = lambda width, cblk: pl.BlockSpec((1, tm, width), lambda b, i: (b, i, cblk))
    gcb = MG_OFF // d
    est = (2 * tm * (F_WIDTH + P_WIDTH + RET_WIDTH) * 2 + 6 * tm * d * 2 + 4 * tm * d * 4
           + (F_WIDTH + P_WIDTH + RET_WIDTH + d) * d * 2 + 4 * tm * d * 4)
    kern = functools.partial(_merge_kernel, final=final)
    return pl.pallas_call(
        kern,
        grid=(bsz, n // tm),
        in_specs=[
            tok(F_WIDTH, 0), tok(P_WIDTH, 0), tok(RET_WIDTH, 0),
            tok(d, gcb), tok(d, gcb + 1), tok(d, gcb + 2),
            tok(d, 0),
            pl.BlockSpec((1, 1, d), lambda b, i: (row(b), 0, 2)),
            _const_spec(w_up_f.shape), _const_spec(w_up_p.shape), _const_spec(w_up_r.shape),
            _const_spec(w_out.shape),
            _const_spec((1, d)),
        ],
        out_specs=pl.BlockSpec((1, tm, d), lambda b, i: (b, i, 0)),
        out_shape=jax.ShapeDtypeStruct((bsz, n, d), F32),
        compiler_params=_params(("arbitrary", "arbitrary"), est),
        name="merge",
    )(u_f, u_p, u_r, proj, proj, proj, x, mod3, w_up_f, w_up_p, w_up_r, w_out, final_g.reshape(1, d))


def kernel(x, c, ctx, c_ctx, w_ada, b_ada, norm_g, w_in, w_fourier, w_pool, pool_scale, ret_decay_logit,
           w_up_fourier, w_up_pool, w_up_ret, w_out, final_norm_g):
    bsz, n, d = x.shape
    depth = w_ada.shape[0]
    assert d == D_MODEL and bsz + 1 <= MOD_ROWS

    cond_rows = jnp.zeros((MOD_ROWS, d), F32).at[:bsz].set(c).at[bsz].set(c_ctx)
    mod = _ada_call(cond_rows, w_ada, b_ada)

    w_in_b = w_in.astype(BF16)
    w_uf_b, w_upl_b = w_up_fourier.astype(BF16), w_up_pool.astype(BF16)
    w_ur_b, w_out_b = w_up_ret.astype(BF16), w_out.astype(BF16)
    s_zero = jnp.zeros((bsz, RET_HEADS, RET_DK, RET_DK), F32)

    for l in range(depth):
        last = l == depth - 1
        mod3 = mod[l].reshape(MOD_ROWS, 1, 3 * d)
        if last:
            proj_c = _inproj_call(ctx, norm_g[l], mod3, bsz, w_in_b[l], R_Q_OFF, MG_OFF - R_Q_OFF)
            s_f, s_b = _retention_call(proj_c, 0, ret_decay_logit[l], s_zero, s_zero,
                                       rope=False, want_out=False, want_states=True)
        else:
            proj_c = _inproj_call(ctx, norm_g[l], mod3, bsz, w_in_b[l], 0, IN_WIDTH)
            uf_c = _fourier_call(proj_c, w_fourier[l])
            up_c = _pool_call(proj_c, w_pool[l], pool_scale[l])
            ur_c, s_f, s_b = _retention_call(proj_c, R_Q_OFF, ret_decay_logit[l], s_zero, s_zero,
                                             rope=False, want_out=True, want_states=True)
            ctx = _merge_call(uf_c, up_c, ur_c, proj_c, ctx, mod3, bsz, w_uf_b[l], w_upl_b[l], w_ur_b[l],
                              w_out_b[l], final_norm_g, False)
        proj = _inproj_call(x, norm_g[l], mod3, None, w_in_b[l], 0, IN_WIDTH)
        u_f = _fourier_call(proj, w_fourier[l])
        u_p = _pool_call(proj, w_pool[l], pool_scale[l])
        (u_r,) = _retention_call(proj, R_Q_OFF, ret_decay_logit[l], s_f, s_b,
                                 rope=True, want_out=True, want_states=False)
        x = _merge_call(u_f, u_p, u_r, proj, x, mod3, None, w_uf_b[l], w_upl_b[l], w_ur_b[l],
                        w_out_b[l], final_norm_g, last)
    return x
```

```python
import functools
import math

import numpy as np
import jax
import jax.numpy as jnp
from jax import lax
from jax.experimental import pallas as pl
from jax.experimental.pallas import tpu as pltpu

F32 = jnp.float32
BF16 = jnp.bfloat16

D_MODEL = 2048
GRID_W = 64
F_GROUPS = 4
F_WIDTH = D_MODEL // 4
F_GDIM = F_WIDTH // F_GROUPS
POOL_WINDOWS = (2, 4, 8, 16)
P_WIDTH = D_MODEL // 4
P_GDIM = P_WIDTH // len(POOL_WINDOWS)
RET_WIDTH = D_MODEL // 2
RET_HEADS = 8
RET_DK = RET_WIDTH // RET_HEADS
N_BRANCH = 3
ROPE_BASE = 10000.0
EPS = 1e-6

F_X_OFF = 0
F_G_OFF = F_X_OFF + F_WIDTH
P_X_OFF = F_G_OFF + F_WIDTH
P_G_OFF = P_X_OFF + P_WIDTH
R_Q_OFF = P_G_OFF + P_WIDTH
R_K_OFF = R_Q_OFF + RET_WIDTH
R_V_OFF = R_K_OFF + RET_WIDTH
R_G_OFF = R_V_OFF + RET_WIDTH
MG_OFF = R_G_OFF + RET_WIDTH
IN_WIDTH = MG_OFF + N_BRANCH * D_MODEL

V7X_VMEM_BYTES = 64 * 1024 * 1024
VMEM_CAP_BYTES = V7X_VMEM_BYTES - 8 * 1024 * 1024
MOD_ROWS = 8
POOL_HALO = 16
RET_CHUNK = 256
IN_TM, IN_TN = 1024, 1024
MERGE_TM = 512
DFT_TM = 256
DFT_RADIX = 4
POOL_TM = 256


def _vmem_limit(estimate_bytes):
    return int(min(max(estimate_bytes * 5 // 4 + (4 << 20), 32 << 20), VMEM_CAP_BYTES))


def _params(sem, estimate_bytes):
    return pltpu.CompilerParams(dimension_semantics=sem, vmem_limit_bytes=_vmem_limit(estimate_bytes))


def _const_spec(shape):
    nd = len(shape)
    return pl.BlockSpec(shape, lambda *_: (0,) * nd, pipeline_mode=pl.Buffered(1))


def _layer_spec(shape, layer):
    return pl.BlockSpec((None,) + tuple(shape[1:]), lambda *_: (layer, 0, 0), pipeline_mode=pl.Buffered(1))


def _silu(v):
    return v * jax.nn.sigmoid(v)


def _dot(a, b):
    return jnp.dot(a, b, preferred_element_type=F32)


def _ada_kernel(s_ref, w_ref, b_ref, o_ref):
    s = _silu(s_ref[...])
    o_ref[0] = _dot(s.astype(BF16), w_ref[0].astype(BF16)) + b_ref[0]


def _ada_call(cond_rows, w_ada, b_ada):
    depth, d, w3 = w_ada.shape
    tn = 1024
    return pl.pallas_call(
        _ada_kernel,
        grid=(depth, w3 // tn),
        in_specs=[
            pl.BlockSpec((MOD_ROWS, d), lambda l, j: (0, 0)),
            pl.BlockSpec((1, d, tn), lambda l, j: (l, 0, j)),
            pl.BlockSpec((1, 1, tn), lambda l, j: (l, 0, j)),
        ],
        out_specs=pl.BlockSpec((1, MOD_ROWS, tn), lambda l, j: (l, 0, j)),
        out_shape=jax.ShapeDtypeStruct((depth, MOD_ROWS, w3), F32),
        compiler_params=_params(("arbitrary", "arbitrary"), 2 * d * tn * 4 + d * tn * 2),
        name="ada_mod",
    )(cond_rows, w_ada, b_ada.reshape(depth, 1, w3))


def _wprep_kernel(w_ref, o_ref, *, qk_lo, qk_hi):
    j = pl.program_id(1)
    is_qk = jnp.logical_and(j >= qk_lo, j < qk_hi)

    @pl.when(jnp.logical_not(is_qk))
    def _():
        o_ref[0] = w_ref[0].astype(BF16)

    @pl.when(is_qk)
    def _():
        quarter = RET_DK // 4
        for hd in range(w_ref.shape[2] // RET_DK):
            sl = slice(hd * RET_DK, (hd + 1) * RET_DK)
            u = w_ref[0, :, sl]
            lane = lax.broadcasted_iota(jnp.int32, u.shape, 1)
            up = pltpu.roll(u, RET_DK - quarter, 1)
            dn = pltpu.roll(u, quarter, 1)
            v = jnp.where(jnp.logical_and(lane >= quarter, lane < 2 * quarter), up,
                          jnp.where(jnp.logical_and(lane >= 2 * quarter, lane < 3 * quarter), dn, u))
            o_ref[0, :, sl] = v.astype(BF16)


def _wprep_call(w_in):
    depth, d, width = w_in.shape
    tn = IN_TN
    assert R_Q_OFF % tn == 0 and R_V_OFF % tn == 0
    kern = functools.partial(_wprep_kernel, qk_lo=R_Q_OFF // tn, qk_hi=R_V_OFF // tn)
    return pl.pallas_call(
        kern,
        grid=(depth, width // tn),
        in_specs=[pl.BlockSpec((1, d, tn), lambda l, j: (l, 0, j))],
        out_specs=pl.BlockSpec((1, d, tn), lambda l, j: (l, 0, j)),
        out_shape=jax.ShapeDtypeStruct((depth, d, width), BF16),
        compiler_params=_params(("arbitrary", "arbitrary"), 2 * d * tn * 6 + d * tn * 8),
        name="w_prep",
    )(w_in)


def _inproj_kernel(x_ref, g_ref, sc_ref, sh_ref, w_ref, o_ref, h_ref):
    @pl.when(pl.program_id(2) == 0)
    def _():
        x = x_ref[0]
        ms = jnp.mean(x * x, axis=-1, keepdims=True)
        y = x * lax.rsqrt(ms + EPS) * g_ref[...]
        h_ref[...] = (y * (1.0 + sc_ref[0]) + sh_ref[0]).astype(BF16)

    o_ref[0] = _dot(h_ref[...], w_ref[...]).astype(o_ref.dtype)


def _inproj_call(x, norm_g, mod3, mod_row, w_bf16, layer, col0, ncols):
    bsz, n, d = x.shape
    tm = min(IN_TM, n)
    tn = IN_TN
    cb0 = col0 // tn
    row = (lambda b: b) if mod_row is None else (lambda b: mod_row)
    est = 2 * tm * d * 4 + 2 * d * tn * 2 + 2 * tm * tn * 2 + tm * d * 2 + tm * tn * 4 + tm * d * 4
    return pl.pallas_call(
        _inproj_kernel,
        grid=(bsz, n // tm, ncols // tn),
        in_specs=[
            pl.BlockSpec((1, tm, d), lambda b, i, j: (b, i, 0)),
            pl.BlockSpec((1, d), lambda b, i, j: (0, 0)),
            pl.BlockSpec((1, 1, d), lambda b, i, j: (row(b), 0, 1)),
            pl.BlockSpec((1, 1, d), lambda b, i, j: (row(b), 0, 0)),
            pl.BlockSpec((None, d, tn), lambda b, i, j: (layer, 0, cb0 + j)),
        ],
        out_specs=pl.BlockSpec((1, tm, tn), lambda b, i, j: (b, i, j)),
        out_shape=jax.ShapeDtypeStruct((bsz, n, ncols), BF16),
        scratch_shapes=[pltpu.VMEM((tm, d), BF16)],
        compiler_params=_params(("arbitrary", "arbitrary", "arbitrary"), est),
        name="in_proj",
    )(x, norm_g.reshape(1, d), mod3, mod3, w_bf16)


@functools.lru_cache(maxsize=None)
def _dft_tables(n, tm, radix):
    m = n // radix
    k = np.arange(m, dtype=np.int64)[None, :]
    r = np.arange(tm, dtype=np.int64)[:, None]
    ang_b = 2.0 * np.pi * ((r * k) % m) / m
    base = (np.arange(m // tm, dtype=np.int64) * tm)[:, None]
    ang_a = 2.0 * np.pi * ((base * k) % m) / m
    rows = np.arange(m, dtype=np.int64)[None, :, None]
    phase = np.arange(radix, dtype=np.int64)[:, None, None]
    ang_t = np.broadcast_to(2.0 * np.pi * ((rows * phase) % n) / n, (radix, m, F_GDIM))
    c = np.arange(F_GDIM, dtype=np.int64)
    ang_c = 2.0 * np.pi * ((c[:, None] * c[None, :]) % F_GDIM) / F_GDIM
    ortho = 1.0 / math.sqrt(n * F_GDIM)
    f = lambda a: np.asarray(a, dtype=np.float32)
    return (f(np.cos(ang_b)), f(np.sin(ang_b)),
            f(np.cos(ang_a))[:, None, :], f(np.sin(ang_a))[:, None, :],
            f(np.cos(ang_t)), f(np.sin(ang_t)),
            f(np.cos(ang_c) * ortho), f(np.sin(ang_c) * ortho))


def _unit_combine(terms):
    acc = None
    for coef, arr in terms:
        if abs(coef) < 1e-9:
            continue
        if abs(abs(coef) - 1.0) < 1e-9:
            term, neg = arr, coef < 0
        else:
            term, neg = arr * coef, False
        if acc is None:
            acc = -term if neg else term
        else:
            acc = acc - term if neg else acc + term
    return acc


def _fourier_kernel(*refs, bsz, tm, radix):
    cb_ref, sb_ref, ca_ref, sa_ref, twc_ref, tws_ref, cc_ref, sc_ref, wf_ref = refs[:9]
    x_refs = refs[9:9 + bsz * radix]
    fg_ref, o_ref, lhs_ref, ab_ref = refs[9 + bsz * radix:]
    gd = F_GDIM

    @pl.when(pl.program_id(0) == 0)
    def _():
        for g in range(F_GROUPS):
            w = wf_ref[g]
            ab_ref[g, :gd, :] = jnp.dot(cc_ref[...], w, preferred_element_type=F32,
                                        precision=lax.Precision.HIGHEST).astype(BF16)
            ab_ref[g, gd:, :] = (-jnp.dot(sc_ref[...], w, preferred_element_type=F32,
                                          precision=lax.Precision.HIGHEST)).astype(BF16)

    ca, sa = ca_ref[0], sa_ref[0]
    cb, sb = cb_ref[...], sb_ref[...]
    lhs_ref[:tm, :] = (ca * cb - sa * sb).astype(BF16)
    lhs_ref[tm:, :] = (sa * cb + ca * sb).astype(BF16)

    for b in range(bsz):
        ec, es = [], []
        for r in range(radix):
            p = _dot(lhs_ref[...], x_refs[b * radix + r][0])
            ec.append(p[:tm])
            es.append(p[tm:])
        for g in range(F_GROUPS):
            sl = slice(g * gd, (g + 1) * gd)
            tc, ts = [ec[0][:, sl]], [es[0][:, sl]]
            for r in range(1, radix):
                c, s = twc_ref[r], tws_ref[r]
                tc.append(c * ec[r][:, sl] - s * es[r][:, sl])
                ts.append(c * es[r][:, sl] + s * ec[r][:, sl])
            for q in range(radix):
                rot = [(math.cos(2.0 * math.pi * q * r / radix), math.sin(2.0 * math.pi * q * r / radix))
                       for r in range(radix)]
                pc = _unit_combine([(a, tc[r]) for r, (a, _) in enumerate(rot)]
                                   + [(-bb, ts[r]) for r, (_, bb) in enumerate(rot)])
                ps = _unit_combine([(a, ts[r]) for r, (a, _) in enumerate(rot)]
                                   + [(bb, tc[r]) for r, (_, bb) in enumerate(rot)])
                cat = jnp.concatenate([pc, ps], axis=1).astype(BF16)
                y = _dot(cat, ab_ref[g])
                o_ref[b, q, :, sl] = (y * _silu(fg_ref[b, q, :, sl].astype(F32))).astype(o_ref.dtype)


def _fourier_call(proj, w_fourier):
    bsz, n, width = proj.shape
    radix = DFT_RADIX
    m = n // radix
    tm = min(DFT_TM, m)
    cb, sb, ca, sa, twc, tws, cc, sc = _dft_tables(n, tm, radix)
    kern = functools.partial(_fourier_kernel, bsz=bsz, tm=tm, radix=radix)
    phases = proj.reshape(bsz, m, radix * width)
    x_specs = [pl.BlockSpec((1, m, F_WIDTH),
                            functools.partial(lambda i, b, r: (b, 0, (r * width + F_X_OFF) // F_WIDTH), b=b, r=r),
                            pipeline_mode=pl.Buffered(1))
               for b in range(bsz) for r in range(radix)]
    quarters = proj.reshape(bsz, radix, m, width)
    est = (2 * tm * m * 4 + bsz * n * F_WIDTH * 2 + 2 * tm * m * 2 + 4 * bsz * radix * tm * F_WIDTH * 2
           + 4 * radix * tm * F_GDIM * 4 + (2 * radix + 4) * 2 * tm * F_WIDTH * 4 + 3 * tm * m * 4)
    out = pl.pallas_call(
        kern,
        grid=(m // tm,),
        in_specs=[
            _const_spec((tm, m)), _const_spec((tm, m)),
            pl.BlockSpec((1, 1, m), lambda i: (i, 0, 0)),
            pl.BlockSpec((1, 1, m), lambda i: (i, 0, 0)),
            pl.BlockSpec((radix, tm, F_GDIM), lambda i: (0, i, 0)),
            pl.BlockSpec((radix, tm, F_GDIM), lambda i: (0, i, 0)),
            _const_spec((F_GDIM, F_GDIM)), _const_spec((F_GDIM, F_GDIM)),
            _const_spec((F_GROUPS, F_GDIM, F_GDIM)),
            *x_specs,
            pl.BlockSpec((bsz, radix, tm, F_WIDTH), lambda i: (0, 0, i, F_G_OFF // F_WIDTH)),
        ],
        out_specs=pl.BlockSpec((bsz, radix, tm, F_WIDTH), lambda i: (0, 0, i, 0)),
        out_shape=jax.ShapeDtypeStruct((bsz, radix, m, F_WIDTH), BF16),
        scratch_shapes=[pltpu.VMEM((2 * tm, m), BF16),
                        pltpu.VMEM((F_GROUPS, 2 * F_GDIM, F_GDIM), BF16)],
        compiler_params=_params(("arbitrary",), est),
        name="fourier",
    )(cb, sb, ca, sa, twc, tws, cc, sc, w_fourier, *([phases] * (bsz * radix)), quarters)
    return out.reshape(bsz, n, F_WIDTH)


@functools.lru_cache(maxsize=None)
def _pool_bands(tm):
    r = np.arange(tm)[:, None]
    main, prev, nxt = [], [], []
    for w in POOL_WINDOWS:
        hw = w // 2
        band = lambda d: ((d >= -hw) & (d <= hw - 1)).astype(np.float32)
        main.append(band(np.arange(tm)[None, :] - r))
        prev.append(band(np.arange(POOL_HALO)[None, :] - POOL_HALO - r))
        nxt.append(band(np.arange(POOL_HALO)[None, :] + tm - r))
    return np.stack(main), np.stack(prev), np.stack(nxt)


def _pool_kernel(xm_ref, xp_ref, xn_ref, pg_ref, bm_ref, bp_ref, bn_ref, w_ref, ps_ref, o_ref, *, n, tm):
    i = pl.program_id(1)
    has_prev = jnp.where(i > 0, 1.0, 0.0).astype(F32)
    has_next = jnp.where(i < pl.num_programs(1) - 1, 1.0, 0.0).astype(F32)
    t = i * tm + lax.broadcasted_iota(jnp.int32, (tm, P_GDIM), 0)
    for g, w in enumerate(POOL_WINDOWS):
        sl = slice(g * P_GDIM, (g + 1) * P_GDIM)
        xg = xm_ref[0, :, sl]
        s = _dot(bm_ref[g], xg)
        s = s + has_prev * _dot(bp_ref[g], xp_ref[0, :, sl])
        s = s + has_next * _dot(bn_ref[g], xn_ref[0, :, sl])
        cnt = (jnp.minimum(t + w // 2, n) - jnp.maximum(t - w // 2, 0)).astype(F32)
        pooled = s / cnt - xg.astype(F32)
        y = _dot(pooled.astype(BF16), w_ref[g].astype(BF16)) * ps_ref[:, sl]
        o_ref[0, :, sl] = (y * _silu(pg_ref[0, :, sl].astype(F32))).astype(o_ref.dtype)


def _pool_call(proj, w_pool, pool_scale):
    bsz, n, _ = proj.shape
    tm = min(POOL_TM, n)
    bands = [jnp.asarray(a, dtype=BF16) for a in _pool_bands(tm)]
    hb = tm // POOL_HALO
    nhb = n // POOL_HALO
    xcol = P_X_OFF // P_WIDTH
    kern = functools.partial(_pool_kernel, n=n, tm=tm)
    return pl.pallas_call(
        kern,
        grid=(bsz, n // tm),
        in_specs=[
            pl.BlockSpec((1, tm, P_WIDTH), lambda b, i: (b, i, xcol)),
            pl.BlockSpec((1, POOL_HALO, P_WIDTH), lambda b, i: (b, jnp.maximum(i * hb - 1, 0), xcol)),
            pl.BlockSpec((1, POOL_HALO, P_WIDTH), lambda b, i: (b, jnp.minimum((i + 1) * hb, nhb - 1), xcol)),
            pl.BlockSpec((1, tm, P_WIDTH), lambda b, i: (b, i, P_G_OFF // P_WIDTH)),
            _const_spec(bands[0].shape), _const_spec(bands[1].shape), _const_spec(bands[2].shape),
            _const_spec(w_pool.shape),
            _const_spec((1, P_WIDTH)),
        ],
        out_specs=pl.BlockSpec((1, tm, P_WIDTH), lambda b, i: (b, i, 0)),
        out_shape=jax.ShapeDtypeStruct((bsz, n, P_WIDTH), BF16),
        compiler_params=_params(("arbitrary", "arbitrary"), 16 << 20),
        name="pool",
    )(proj, proj, proj, proj, *bands, w_pool, pool_scale.reshape(1, P_WIDTH))


@functools.lru_cache(maxsize=None)
def _rope_tables(n):
    pos = np.arange(n)
    nf = RET_DK // 4
    inv = ROPE_BASE ** (-np.arange(nf, dtype=np.float64) / nf)
    ang_r = (pos // GRID_W)[:, None] * inv[None, :]
    ang_c = (pos % GRID_W)[:, None] * inv[None, :]
    cos = np.concatenate([np.cos(ang_r), np.cos(ang_c)] * 2, axis=1)
    sin = np.concatenate([-np.sin(ang_r), -np.sin(ang_c), np.sin(ang_r), np.sin(ang_c)], axis=1)
    return np.asarray(cos, np.float32), np.asarray(sin, np.float32)


def _log_sigmoid(v):
    return jnp.minimum(v, 0.0) - jnp.log1p(jnp.exp(-jnp.abs(v)))


def _retention_kernel(*refs, n, rope, want_out, want_states):
    it = iter(refs)
    dl_ref = next(it)
    q_ref, k_ref, v_ref = next(it), next(it), next(it)
    rg_ref = next(it) if want_out else None
    cos_ref, sin_ref = (next(it), next(it)) if rope else (None, None)
    s0f_ref, s0b_ref = next(it), next(it)
    o_ref = next(it) if want_out else None
    sfo_ref, sbo_ref = (next(it), next(it)) if want_states else (None, None)
    kv_s, st_s, m_s, d_s = next(it), next(it), next(it), next(it)
    q_s, k_s = (next(it), next(it)) if rope else (None, None)

    ch, dk = RET_CHUNK, RET_DK
    nc = n // ch
    unroll = min(2, nc)
    h = pl.program_id(1)
    scale = dk ** -0.5

    def log_gamma(direction, shape):
        return _log_sigmoid(jnp.full(shape, dl_ref[direction, h], F32))

    lg_f, lg_b = log_gamma(0, (ch, dk)), log_gamma(1, (ch, dk))
    row = lax.broadcasted_iota(jnp.int32, (ch, dk), 0).astype(F32)
    d_s[0] = (jnp.exp(lg_f * (row + 1.0)) * scale).astype(BF16)
    d_s[1] = (jnp.exp(lg_b * (ch - row)) * scale).astype(BF16)
    d_s[2] = jnp.exp(lg_f * (ch - 1.0 - row)).astype(BF16)
    d_s[3] = jnp.exp(lg_b * row).astype(BF16)
    gf_c = jnp.exp(log_gamma(0, (dk, dk)) * ch)
    gb_c = jnp.exp(log_gamma(1, (dk, dk)) * ch)
    if want_out:
        diff = (lax.broadcasted_iota(jnp.int32, (ch, ch), 0)
                - lax.broadcasted_iota(jnp.int32, (ch, ch), 1)).astype(F32)
        lf, lb = log_gamma(0, (ch, ch)), log_gamma(1, (ch, ch))
        m_s[...] = scale * (jnp.where(diff >= 0, jnp.exp(lf * jnp.maximum(diff, 0.0)), 0.0)
                            + jnp.where(diff <= 0, jnp.exp(lb * jnp.maximum(-diff, 0.0)), 0.0))

    def chunk(c):
        return pl.ds(pl.multiple_of(c * ch, ch), ch)

    def rotate(u, cos, sin):
        u = u.astype(F32)
        return (u * cos + pltpu.roll(u, dk // 2, 1) * sin).astype(BF16)

    def chunk_states(c, carry):
        sl = chunk(c)
        k = k_ref[0, sl, :]
        if rope:
            cos, sin = cos_ref[sl, :], sin_ref[sl, :]
            k = rotate(k, cos, sin)
            k_s[sl, :] = k
            if want_out:
                q_s[sl, :] = rotate(q_ref[0, sl, :], cos, sin)
        kd = jnp.concatenate([k * d_s[2], k * d_s[3]], axis=1)
        kv_s[c] = lax.dot_general(kd, v_ref[0, sl, :], (((0,), (0,)), ((), ())), preferred_element_type=F32)
        return carry

    lax.fori_loop(0, nc, chunk_states, 0, unroll=unroll)

    def scan_fwd(c, s):
        st_s[c, :dk, :] = s.astype(BF16)
        return gf_c * s + kv_s[c, :dk, :]

    def scan_bwd(t, s):
        c = nc - 1 - t
        st_s[c, dk:, :] = s.astype(BF16)
        return gb_c * s + kv_s[c, dk:, :]

    s_f = lax.fori_loop(0, nc, scan_fwd, s0f_ref[0, 0])
    s_b = lax.fori_loop(0, nc, scan_bwd, s0b_ref[0, 0])
    if want_states:
        sfo_ref[0, 0] = s_f
        sbo_ref[0, 0] = s_b

    def chunk_outputs(c, carry):
        sl = chunk(c)
        q = q_s[sl, :] if rope else q_ref[0, sl, :]
        k = k_s[sl, :] if rope else k_ref[0, sl, :]
        s = lax.dot_general(q, k, (((1,), (1,)), ((), ())), preferred_element_type=F32)
        o = _dot((s * m_s[...]).astype(BF16), v_ref[0, sl, :])
        qd = jnp.concatenate([q * d_s[0], q * d_s[1]], axis=1)
        o = o + _dot(qd, st_s[c])
        o = o * lax.rsqrt(jnp.mean(o * o, axis=-1, keepdims=True) + EPS)
        o_ref[0, sl, :] = (o * _silu(rg_ref[0, sl, :].astype(F32))).astype(o_ref.dtype)
        return carry

    if want_out:
        lax.fori_loop(0, nc, chunk_outputs, 0, unroll=unroll)


def _retention_call(proj, q_off, decay_logit, s0_f, s0_b, *, rope, want_out, want_states):
    bsz, n, _ = proj.shape
    nh, dk, ch = RET_HEADS, RET_DK, RET_CHUNK
    assert n % ch == 0
    cb = q_off // dk
    col = lambda j: pl.BlockSpec((1, n, dk), functools.partial(lambda b, h, j: (b, 0, cb + j * nh + h), j=j))
    st_spec = pl.BlockSpec((1, 1, dk, dk), lambda b, h: (b, h, 0, 0))
    in_specs = [pl.BlockSpec(memory_space=pltpu.SMEM), col(0), col(1), col(2)]
    args = [decay_logit, proj, proj, proj]
    if want_out:
        in_specs.append(col(3))
        args.append(proj)
    if rope:
        cos, sin = _rope_tables(n)
        in_specs += [_const_spec((n, dk)), _const_spec((n, dk))]
        args += [cos, sin]
    in_specs += [st_spec, st_spec]
    args += [s0_f, s0_b]
    out_specs, out_shape = [], []
    if want_out:
        out_specs.append(pl.BlockSpec((1, n, dk), lambda b, h: (b, 0, h)))
        out_shape.append(jax.ShapeDtypeStruct((bsz, n, RET_WIDTH), BF16))
    if want_states:
        out_specs += [st_spec, st_spec]
        out_shape += [jax.ShapeDtypeStruct((bsz, nh, dk, dk), F32)] * 2
    kern = functools.partial(_retention_kernel, n=n, rope=rope, want_out=want_out, want_states=want_states)
    nc = n // ch
    scratch = [pltpu.VMEM((nc, 2 * dk, dk), F32), pltpu.VMEM((nc, 2 * dk, dk), BF16),
               pltpu.VMEM((ch, ch), F32), pltpu.VMEM((4, ch, dk), BF16)]
    if rope:
        scratch += [pltpu.VMEM((n, dk), BF16), pltpu.VMEM((n, dk), BF16)]
    est = (12 * n * dk * 2 + 2 * n * dk * 4 + nc * 2 * dk * dk * 6 + 16 * ch * ch * 4)
    return pl.pallas_call(
        kern,
        grid=(bsz, nh),
        in_specs=in_specs,
        out_specs=out_specs,
        out_shape=out_shape,
        scratch_shapes=scratch,
        compiler_params=_params(("arbitrary", "arbitrary"), est),
        name="retention",
    )(*args)


def _merge_kernel(uf_ref, up_ref, ur_ref, g0_ref, g1_ref, g2_ref, x_ref, gate_ref,
                  wf_ref, wp_ref, wr_ref, wo_ref, fg_ref, o_ref, *, final):
    m = jax.nn.sigmoid(g0_ref[0].astype(F32)) * _dot(uf_ref[0], wf_ref[...])
    m = m + jax.nn.sigmoid(g1_ref[0].astype(F32)) * _dot(up_ref[0], wp_ref[...])
    m = m + jax.nn.sigmoid(g2_ref[0].astype(F32)) * _dot(ur_ref[0], wr_ref[...])
    y = _dot(m.astype(BF16), wo_ref[...])
    xn = x_ref[0] + gate_ref[0] * y
    if final:
        ms = jnp.mean(xn * xn, axis=-1, keepdims=True)
        xn = xn * lax.rsqrt(ms + EPS) * fg_ref[...]
    o_ref[0] = xn


def _merge_call(u_f, u_p, u_r, proj, x, mod3, mod_row, w_up_f, w_up_p, w_up_r, w_out, layer, final_g, final):
    bsz, n, d = x.shape
    tm = min(MERGE_TM, n)
    row = (lambda b: b) if mod_row is None else (lambda b: mod_row)
    tok = lambda width, cblk: pl.BlockSpec((1, tm, width), lambda b, i: (b, i, cblk))
    gcb = MG_OFF // d
    est = (2 * tm * (F_WIDTH + P_WIDTH + RET_WIDTH) * 2 + 6 * tm * d * 2 + 4 * tm * d * 4
           + (F_WIDTH + P_WIDTH + RET_WIDTH + d) * d * 2 + 4 * tm * d * 4)
    kern = functools.partial(_merge_kernel, final=final)
    return pl.pallas_call(
        kern,
        grid=(bsz, n // tm),
        in_specs=[
            tok(F_WIDTH, 0), tok(P_WIDTH, 0), tok(RET_WIDTH, 0),
            tok(d, gcb), tok(d, gcb + 1), tok(d, gcb + 2),
            tok(d, 0),
            pl.BlockSpec((1, 1, d), lambda b, i: (row(b), 0, 2)),
            _layer_spec(w_up_f.shape, layer), _layer_spec(w_up_p.shape, layer), _layer_spec(w_up_r.shape, layer),
            _layer_spec(w_out.shape, layer),
            _const_spec((1, d)),
        ],
        out_specs=pl.BlockSpec((1, tm, d), lambda b, i: (b, i, 0)),
        out_shape=jax.ShapeDtypeStruct((bsz, n, d), F32),
        compiler_params=_params(("arbitrary", "arbitrary"), est),
        name="merge",
    )(u_f, u_p, u_r, proj, proj, proj, x, mod3, w_up_f, w_up_p, w_up_r, w_out, final_g.reshape(1, d))


def kernel(x, c, ctx, c_ctx, w_ada, b_ada, norm_g, w_in, w_fourier, w_pool, pool_scale, ret_decay_logit,
           w_up_fourier, w_up_pool, w_up_ret, w_out, final_norm_g):
    bsz, n, d = x.shape
    depth = w_ada.shape[0]
    assert d == D_MODEL and bsz + 1 <= MOD_ROWS

    cond_rows = jnp.zeros((MOD_ROWS, d), F32).at[:bsz].set(c).at[bsz].set(c_ctx)
    mod = _ada_call(cond_rows, w_ada, b_ada)

    w_in_b = _wprep_call(w_in)
    w_uf_b, w_upl_b = w_up_fourier.astype(BF16), w_up_pool.astype(BF16)
    w_ur_b, w_out_b = w_up_ret.astype(BF16), w_out.astype(BF16)
    s_zero = jnp.zeros((bsz, RET_HEADS, RET_DK, RET_DK), F32)

    lc = ctx.shape[1]

    def ctx_inproj(l, mod3, col0, ncols):
        p = _inproj_call(ctx.reshape(1, bsz * lc, d), norm_g[l], mod3, bsz, w_in_b, l, col0, ncols)
        return p.reshape(bsz, lc, ncols)

    for l in range(depth):
        last = l == depth - 1
        mod3 = mod[l].reshape(MOD_ROWS, 1, 3 * d)
        if last:
            proj_c = ctx_inproj(l, mod3, R_Q_OFF, MG_OFF - R_Q_OFF)
            s_f, s_b = _retention_call(proj_c, 0, ret_decay_logit[l], s_zero, s_zero,
                                       rope=False, want_out=False, want_states=True)
        else:
            proj_c = ctx_inproj(l, mod3, 0, IN_WIDTH)
            uf_c = _fourier_call(proj_c, w_fourier[l])
            up_c = _pool_call(proj_c, w_pool[l], pool_scale[l])
            ur_c, s_f, s_b = _retention_call(proj_c, R_Q_OFF, ret_decay_logit[l], s_zero, s_zero,
                                             rope=False, want_out=True, want_states=True)
            ctx = _merge_call(uf_c, up_c, ur_c, proj_c, ctx, mod3, bsz, w_uf_b, w_upl_b, w_ur_b,
                              w_out_b, l, final_norm_g, False)
        proj = _inproj_call(x, norm_g[l], mod3, None, w_in_b, l, 0, IN_WIDTH)
        u_f = _fourier_call(proj, w_fourier[l])
        u_p = _pool_call(proj, w_pool[l], pool_scale[l])
        (u_r,) = _retention_call(proj, R_Q_OFF, ret_decay_logit[l], s_f, s_b,
                                 rope=True, want_out=True, want_states=False)
        x = _merge_call(u_f, u_p, u_r, proj, x, mod3, None, w_uf_b, w_upl_b, w_ur_b,
                        w_out_b, l, final_norm_g, last)
    return x
```

```python
import functools
import math

import numpy as np
import jax
import jax.numpy as jnp
from jax import lax
from jax.experimental import pallas as pl
from jax.experimental.pallas import tpu as pltpu

F32 = jnp.float32
BF16 = jnp.bfloat16

D_MODEL = 2048
GRID_W = 64
F_GROUPS = 4
F_WIDTH = D_MODEL // 4
F_GDIM = F_WIDTH // F_GROUPS
POOL_WINDOWS = (2, 4, 8, 16)
P_WIDTH = D_MODEL // 4
P_GDIM = P_WIDTH // len(POOL_WINDOWS)
RET_WIDTH = D_MODEL // 2
RET_HEADS = 8
RET_DK = RET_WIDTH // RET_HEADS
N_BRANCH = 3
ROPE_BASE = 10000.0
EPS = 1e-6

F_X_OFF = 0
F_G_OFF = F_X_OFF + F_WIDTH
P_X_OFF = F_G_OFF + F_WIDTH
P_G_OFF = P_X_OFF + P_WIDTH
R_Q_OFF = P_G_OFF + P_WIDTH
R_K_OFF = R_Q_OFF + RET_WIDTH
R_V_OFF = R_K_OFF + RET_WIDTH
R_G_OFF = R_V_OFF + RET_WIDTH
MG_OFF = R_G_OFF + RET_WIDTH
IN_WIDTH = MG_OFF + N_BRANCH * D_MODEL

V7X_VMEM_BYTES = 64 * 1024 * 1024
VMEM_CAP_BYTES = V7X_VMEM_BYTES - 8 * 1024 * 1024
MOD_ROWS = 8
POOL_HALO = 16
RET_CHUNK = 256
IN_TM, IN_TN = 1024, 1024
MERGE_TM = 512
DFT_TM = 256
DFT_RADIX = 4
POOL_TM = 256


def _vmem_limit(estimate_bytes):
    return int(min(max(estimate_bytes * 5 // 4 + (4 << 20), 32 << 20), VMEM_CAP_BYTES))


def _params(sem, estimate_bytes):
    return pltpu.CompilerParams(dimension_semantics=sem, vmem_limit_bytes=_vmem_limit(estimate_bytes))


def _const_spec(shape):
    nd = len(shape)
    return pl.BlockSpec(shape, lambda *_: (0,) * nd, pipeline_mode=pl.Buffered(1))


def _layer_spec(shape, layer):
    return pl.BlockSpec((None,) + tuple(shape[1:]), lambda *_: (layer, 0, 0), pipeline_mode=pl.Buffered(1))


def _silu(v):
    return v * jax.nn.sigmoid(v)


def _dot(a, b):
    return jnp.dot(a, b, preferred_element_type=F32)


def _ada_kernel(s_ref, w_ref, b_ref, o_ref):
    s = _silu(s_ref[...])
    o_ref[0] = _dot(s.astype(BF16), w_ref[0].astype(BF16)) + b_ref[0]


def _ada_call(cond_rows, w_ada, b_ada):
    depth, d, w3 = w_ada.shape
    tn = 1024
    return pl.pallas_call(
        _ada_kernel,
        grid=(depth, w3 // tn),
        in_specs=[
            pl.BlockSpec((MOD_ROWS, d), lambda l, j: (0, 0)),
            pl.BlockSpec((1, d, tn), lambda l, j: (l, 0, j)),
            pl.BlockSpec((1, 1, tn), lambda l, j: (l, 0, j)),
        ],
        out_specs=pl.BlockSpec((1, MOD_ROWS, tn), lambda l, j: (l, 0, j)),
        out_shape=jax.ShapeDtypeStruct((depth, MOD_ROWS, w3), F32),
        compiler_params=_params(("arbitrary", "arbitrary"), 2 * d * tn * 4 + d * tn * 2),
        name="ada_mod",
    )(cond_rows, w_ada, b_ada.reshape(depth, 1, w3))


def _wprep_kernel(w_ref, o_ref, *, qk_lo, qk_hi):
    j = pl.program_id(1)
    is_qk = jnp.logical_and(j >= qk_lo, j < qk_hi)

    @pl.when(jnp.logical_not(is_qk))
    def _():
        o_ref[0] = w_ref[0].astype(BF16)

    @pl.when(is_qk)
    def _():
        quarter = RET_DK // 4
        for hd in range(w_ref.shape[2] // RET_DK):
            sl = slice(hd * RET_DK, (hd + 1) * RET_DK)
            u = w_ref[0, :, sl]
            lane = lax.broadcasted_iota(jnp.int32, u.shape, 1)
            up = pltpu.roll(u, RET_DK - quarter, 1)
            dn = pltpu.roll(u, quarter, 1)
            v = jnp.where(jnp.logical_and(lane >= quarter, lane < 2 * quarter), up,
                          jnp.where(jnp.logical_and(lane >= 2 * quarter, lane < 3 * quarter), dn, u))
            o_ref[0, :, sl] = v.astype(BF16)


def _wprep_call(w_in):
    depth, d, width = w_in.shape
    tn = IN_TN
    assert R_Q_OFF % tn == 0 and R_V_OFF % tn == 0
    kern = functools.partial(_wprep_kernel, qk_lo=R_Q_OFF // tn, qk_hi=R_V_OFF // tn)
    return pl.pallas_call(
        kern,
        grid=(depth, width // tn),
        in_specs=[pl.BlockSpec((1, d, tn), lambda l, j: (l, 0, j))],
        out_specs=pl.BlockSpec((1, d, tn), lambda l, j: (l, 0, j)),
        out_shape=jax.ShapeDtypeStruct((depth, d, width), BF16),
        compiler_params=_params(("arbitrary", "arbitrary"), 2 * d * tn * 6 + d * tn * 8),
        name="w_prep",
    )(w_in)


def _inproj_kernel(x_ref, g_ref, sc_ref, sh_ref, w_ref, o_ref, h_ref):
    @pl.when(pl.program_id(2) == 0)
    def _():
        x = x_ref[0]
        ms = jnp.mean(x * x, axis=-1, keepdims=True)
        y = x * lax.rsqrt(ms + EPS) * g_ref[...]
        h_ref[...] = (y * (1.0 + sc_ref[0]) + sh_ref[0]).astype(BF16)

    o_ref[0] = _dot(h_ref[...], w_ref[...]).astype(o_ref.dtype)


def _inproj_call(x, norm_g, mod3, mod_row, w_bf16, layer, col0, ncols):
    bsz, n, d = x.shape
    tm = min(IN_TM, n)
    tn = IN_TN
    cb0 = col0 // tn
    row = (lambda b: b) if mod_row is None else (lambda b: mod_row)
    est = 2 * tm * d * 4 + 2 * d * tn * 2 + 2 * tm * tn * 2 + tm * d * 2 + tm * tn * 4 + tm * d * 4
    return pl.pallas_call(
        _inproj_kernel,
        grid=(bsz, n // tm, ncols // tn),
        in_specs=[
            pl.BlockSpec((1, tm, d), lambda b, i, j: (b, i, 0)),
            pl.BlockSpec((1, d), lambda b, i, j: (0, 0)),
            pl.BlockSpec((1, 1, d), lambda b, i, j: (row(b), 0, 1)),
            pl.BlockSpec((1, 1, d), lambda b, i, j: (row(b), 0, 0)),
            pl.BlockSpec((None, d, tn), lambda b, i, j: (layer, 0, cb0 + j)),
        ],
        out_specs=pl.BlockSpec((1, tm, tn), lambda b, i, j: (b, i, j)),
        out_shape=jax.ShapeDtypeStruct((bsz, n, ncols), BF16),
        scratch_shapes=[pltpu.VMEM((tm, d), BF16)],
        compiler_params=_params(("arbitrary", "arbitrary", "arbitrary"), est),
        name="in_proj",
    )(x, norm_g.reshape(1, d), mod3, mod3, w_bf16)


@functools.lru_cache(maxsize=None)
def _dft_tables(n, tm, radix):
    m = n // radix
    k = np.arange(m, dtype=np.int64)[None, :]
    r = np.arange(tm, dtype=np.int64)[:, None]
    ang_b = 2.0 * np.pi * ((r * k) % m) / m
    base = (np.arange(m // tm, dtype=np.int64) * tm)[:, None]
    ang_a = 2.0 * np.pi * ((base * k) % m) / m
    rows = np.arange(m, dtype=np.int64)[None, :, None]
    phase = np.arange(radix, dtype=np.int64)[:, None, None]
    ang_t = np.broadcast_to(2.0 * np.pi * ((rows * phase) % n) / n, (radix, m, F_GDIM))
    c = np.arange(F_GDIM, dtype=np.int64)
    ang_c = 2.0 * np.pi * ((c[:, None] * c[None, :]) % F_GDIM) / F_GDIM
    ortho = 1.0 / math.sqrt(n * F_GDIM)
    f = lambda a: np.asarray(a, dtype=np.float32)
    return (f(np.cos(ang_b)), f(np.sin(ang_b)),
            f(np.cos(ang_a))[:, None, :], f(np.sin(ang_a))[:, None, :],
            f(np.cos(ang_t)), f(np.sin(ang_t)),
            f(np.cos(ang_c) * ortho), f(np.sin(ang_c) * ortho))


def _unit_combine(terms):
    acc = None
    for coef, arr in terms:
        if abs(coef) < 1e-9:
            continue
        if abs(abs(coef) - 1.0) < 1e-9:
            term, neg = arr, coef < 0
        else:
            term, neg = arr * coef, False
        if acc is None:
            acc = -term if neg else term
        else:
            acc = acc - term if neg else acc + term
    return acc


def _fourier_kernel(*refs, bsz, tm, radix):
    cb_ref, sb_ref, ca_ref, sa_ref, twc_ref, tws_ref, cc_ref, sc_ref, wf_ref = refs[:9]
    x_refs = refs[9:9 + bsz * radix]
    fg_ref, o_ref, lhs_ref, ab_ref = refs[9 + bsz * radix:]
    gd = F_GDIM

    @pl.when(pl.program_id(0) == 0)
    def _():
        for g in range(F_GROUPS):
            w = wf_ref[g]
            ab_ref[g, :gd, :] = jnp.dot(cc_ref[...], w, preferred_element_type=F32,
                                        precision=lax.Precision.HIGHEST).astype(BF16)
            ab_ref[g, gd:, :] = (-jnp.dot(sc_ref[...], w, preferred_element_type=F32,
                                          precision=lax.Precision.HIGHEST)).astype(BF16)

    ca, sa = ca_ref[0], sa_ref[0]
    cb, sb = cb_ref[...], sb_ref[...]
    lhs_ref[:tm, :] = (ca * cb - sa * sb).astype(BF16)
    lhs_ref[tm:, :] = (sa * cb + ca * sb).astype(BF16)

    for b in range(bsz):
        ec, es = [], []
        for r in range(radix):
            p = _dot(lhs_ref[...], x_refs[b * radix + r][0])
            ec.append(p[:tm])
            es.append(p[tm:])
        for g in range(F_GROUPS):
            sl = slice(g * gd, (g + 1) * gd)
            tc, ts = [ec[0][:, sl]], [es[0][:, sl]]
            for r in range(1, radix):
                c, s = twc_ref[r], tws_ref[r]
                tc.append(c * ec[r][:, sl] - s * es[r][:, sl])
                ts.append(c * es[r][:, sl] + s * ec[r][:, sl])
            for q in range(radix):
                rot = [(math.cos(2.0 * math.pi * q * r / radix), math.sin(2.0 * math.pi * q * r / radix))
                       for r in range(radix)]
                pc = _unit_combine([(a, tc[r]) for r, (a, _) in enumerate(rot)]
                                   + [(-bb, ts[r]) for r, (_, bb) in enumerate(rot)])
                ps = _unit_combine([(a, ts[r]) for r, (a, _) in enumerate(rot)]
                                   + [(bb, tc[r]) for r, (_, bb) in enumerate(rot)])
                cat = jnp.concatenate([pc, ps], axis=1).astype(BF16)
                y = _dot(cat, ab_ref[g])
                o_ref[b, q, :, sl] = (y * _silu(fg_ref[b, q, :, sl].astype(F32))).astype(o_ref.dtype)


def _fourier_call(proj, w_fourier):
    bsz, n, width = proj.shape
    radix = DFT_RADIX
    m = n // radix
    tm = min(DFT_TM, m)
    cb, sb, ca, sa, twc, tws, cc, sc = _dft_tables(n, tm, radix)
    kern = functools.partial(_fourier_kernel, bsz=bsz, tm=tm, radix=radix)
    phases = proj[:, :, F_X_OFF:F_X_OFF + F_WIDTH].reshape(bsz, m, radix, F_WIDTH).transpose(0, 2, 1, 3)
    x_specs = [pl.BlockSpec((1, None, m, F_WIDTH), functools.partial(lambda i, b, r: (b, r, 0, 0), b=b, r=r),
                            pipeline_mode=pl.Buffered(1))
               for b in range(bsz) for r in range(radix)]
    quarters = proj.reshape(bsz, radix, m, width)
    est = (2 * tm * m * 4 + bsz * n * F_WIDTH * 2 + 2 * tm * m * 2 + 4 * bsz * radix * tm * F_WIDTH * 2
           + 4 * radix * tm * F_GDIM * 4 + (2 * radix + 4) * 2 * tm * F_WIDTH * 4 + 3 * tm * m * 4)
    out = pl.pallas_call(
        kern,
        grid=(m // tm,),
        in_specs=[
            _const_spec((tm, m)), _const_spec((tm, m)),
            pl.BlockSpec((1, 1, m), lambda i: (i, 0, 0)),
            pl.BlockSpec((1, 1, m), lambda i: (i, 0, 0)),
            pl.BlockSpec((radix, tm, F_GDIM), lambda i: (0, i, 0)),
            pl.BlockSpec((radix, tm, F_GDIM), lambda i: (0, i, 0)),
            _const_spec((F_GDIM, F_GDIM)), _const_spec((F_GDIM, F_GDIM)),
            _const_spec((F_GROUPS, F_GDIM, F_GDIM)),
            *x_specs,
            pl.BlockSpec((bsz, radix, tm, F_WIDTH), lambda i: (0, 0, i, F_G_OFF // F_WIDTH)),
        ],
        out_specs=pl.BlockSpec((bsz, radix, tm, F_WIDTH), lambda i: (0, 0, i, 0)),
        out_shape=jax.ShapeDtypeStruct((bsz, radix, m, F_WIDTH), BF16),
        scratch_shapes=[pltpu.VMEM((2 * tm, m), BF16),
                        pltpu.VMEM((F_GROUPS, 2 * F_GDIM, F_GDIM), BF16)],
        compiler_params=_params(("arbitrary",), est),
        name="fourier",
    )(cb, sb, ca, sa, twc, tws, cc, sc, w_fourier, *([phases] * (bsz * radix)), quarters)
    return out.reshape(bsz, n, F_WIDTH)


@functools.lru_cache(maxsize=None)
def _pool_bands(tm):
    r = np.arange(tm)[:, None]
    main, prev, nxt = [], [], []
    for w in POOL_WINDOWS:
        hw = w // 2
        band = lambda d: ((d >= -hw) & (d <= hw - 1)).astype(np.float32)
        main.append(band(np.arange(tm)[None, :] - r))
        prev.append(band(np.arange(POOL_HALO)[None, :] - POOL_HALO - r))
        nxt.append(band(np.arange(POOL_HALO)[None, :] + tm - r))
    return np.stack(main), np.stack(prev), np.stack(nxt)


def _pool_kernel(xm_ref, xp_ref, xn_ref, pg_ref, bm_ref, bp_ref, bn_ref, w_ref, ps_ref, o_ref, *, n, tm):
    i = pl.program_id(1)
    has_prev = jnp.where(i > 0, 1.0, 0.0).astype(F32)
    has_next = jnp.where(i < pl.num_programs(1) - 1, 1.0, 0.0).astype(F32)
    t = i * tm + lax.broadcasted_iota(jnp.int32, (tm, P_GDIM), 0)
    for g, w in enumerate(POOL_WINDOWS):
        sl = slice(g * P_GDIM, (g + 1) * P_GDIM)
        xg = xm_ref[0, :, sl]
        s = _dot(bm_ref[g], xg)
        s = s + has_prev * _dot(bp_ref[g], xp_ref[0, :, sl])
        s = s + has_next * _dot(bn_ref[g], xn_ref[0, :, sl])
        cnt = (jnp.minimum(t + w // 2, n) - jnp.maximum(t - w // 2, 0)).astype(F32)
        pooled = s / cnt - xg.astype(F32)
        y = _dot(pooled.astype(BF16), w_ref[g].astype(BF16)) * ps_ref[:, sl]
        o_ref[0, :, sl] = (y * _silu(pg_ref[0, :, sl].astype(F32))).astype(o_ref.dtype)


def _pool_call(proj, w_pool, pool_scale):
    bsz, n, _ = proj.shape
    tm = min(POOL_TM, n)
    bands = [jnp.asarray(a, dtype=BF16) for a in _pool_bands(tm)]
    hb = tm // POOL_HALO
    nhb = n // POOL_HALO
    xcol = P_X_OFF // P_WIDTH
    kern = functools.partial(_pool_kernel, n=n, tm=tm)
    return pl.pallas_call(
        kern,
        grid=(bsz, n // tm),
        in_specs=[
            pl.BlockSpec((1, tm, P_WIDTH), lambda b, i: (b, i, xcol)),
            pl.BlockSpec((1, POOL_HALO, P_WIDTH), lambda b, i: (b, jnp.maximum(i * hb - 1, 0), xcol)),
            pl.BlockSpec((1, POOL_HALO, P_WIDTH), lambda b, i: (b, jnp.minimum((i + 1) * hb, nhb - 1), xcol)),
            pl.BlockSpec((1, tm, P_WIDTH), lambda b, i: (b, i, P_G_OFF // P_WIDTH)),
            _const_spec(bands[0].shape), _const_spec(bands[1].shape), _const_spec(bands[2].shape),
            _const_spec(w_pool.shape),
            _const_spec((1, P_WIDTH)),
        ],
        out_specs=pl.BlockSpec((1, tm, P_WIDTH), lambda b, i: (b, i, 0)),
        out_shape=jax.ShapeDtypeStruct((bsz, n, P_WIDTH), BF16),
        compiler_params=_params(("arbitrary", "arbitrary"), 16 << 20),
        name="pool",
    )(proj, proj, proj, proj, *bands, w_pool, pool_scale.reshape(1, P_WIDTH))


@functools.lru_cache(maxsize=None)
def _rope_tables(n):
    pos = np.arange(n)
    nf = RET_DK // 4
    inv = ROPE_BASE ** (-np.arange(nf, dtype=np.float64) / nf)
    ang_r = (pos // GRID_W)[:, None] * inv[None, :]
    ang_c = (pos % GRID_W)[:, None] * inv[None, :]
    cos = np.concatenate([np.cos(ang_r), np.cos(ang_c)] * 2, axis=1)
    sin = np.concatenate([-np.sin(ang_r), -np.sin(ang_c), np.sin(ang_r), np.sin(ang_c)], axis=1)
    return np.asarray(cos, np.float32), np.asarray(sin, np.float32)


def _log_sigmoid(v):
    return jnp.minimum(v, 0.0) - jnp.log1p(jnp.exp(-jnp.abs(v)))


def _retention_kernel(*refs, n, rope, want_out, want_states):
    it = iter(refs)
    dl_ref = next(it)
    q_ref, k_ref, v_ref = next(it), next(it), next(it)
    rg_ref = next(it) if want_out else None
    cos_ref, sin_ref = (next(it), next(it)) if rope else (None, None)
    s0f_ref, s0b_ref = next(it), next(it)
    o_ref = next(it) if want_out else None
    sfo_ref, sbo_ref = (next(it), next(it)) if want_states else (None, None)
    kv_s, st_s, m_s, d_s = next(it), next(it), next(it), next(it)
    q_s, k_s = (next(it), next(it)) if rope else (None, None)

    ch, dk = RET_CHUNK, RET_DK
    nc = n // ch
    unroll = min(2, nc)
    h = pl.program_id(1)
    scale = dk ** -0.5

    def log_gamma(direction, shape):
        return _log_sigmoid(jnp.full(shape, dl_ref[direction, h], F32))

    lg_f, lg_b = log_gamma(0, (ch, dk)), log_gamma(1, (ch, dk))
    row = lax.broadcasted_iota(jnp.int32, (ch, dk), 0).astype(F32)
    d_s[0] = (jnp.exp(lg_f * (row + 1.0)) * scale).astype(BF16)
    d_s[1] = (jnp.exp(lg_b * (ch - row)) * scale).astype(BF16)
    d_s[2] = jnp.exp(lg_f * (ch - 1.0 - row)).astype(BF16)
    d_s[3] = jnp.exp(lg_b * row).astype(BF16)
    gf_c = jnp.exp(log_gamma(0, (dk, dk)) * ch)
    gb_c = jnp.exp(log_gamma(1, (dk, dk)) * ch)
    if want_out:
        diff = (lax.broadcasted_iota(jnp.int32, (ch, ch), 0)
                - lax.broadcasted_iota(jnp.int32, (ch, ch), 1)).astype(F32)
        lf, lb = log_gamma(0, (ch, ch)), log_gamma(1, (ch, ch))
        m_s[...] = scale * (jnp.where(diff >= 0, jnp.exp(lf * jnp.maximum(diff, 0.0)), 0.0)
                            + jnp.where(diff <= 0, jnp.exp(lb * jnp.maximum(-diff, 0.0)), 0.0))

    def chunk(c):
        return pl.ds(pl.multiple_of(c * ch, ch), ch)

    def rotate(u, cos, sin):
        u = u.astype(F32)
        return (u * cos + pltpu.roll(u, dk // 2, 1) * sin).astype(BF16)

    def chunk_states(c, carry):
        sl = chunk(c)
        k = k_ref[0, sl, :]
        if rope:
            cos, sin = cos_ref[sl, :], sin_ref[sl, :]
            k = rotate(k, cos, sin)
            k_s[sl, :] = k
            if want_out:
                q_s[sl, :] = rotate(q_ref[0, sl, :], cos, sin)
        kd = jnp.concatenate([k * d_s[2], k * d_s[3]], axis=1)
        kv_s[c] = lax.dot_general(kd, v_ref[0, sl, :], (((0,), (0,)), ((), ())), preferred_element_type=F32)
        return carry

    lax.fori_loop(0, nc, chunk_states, 0, unroll=unroll)

    def scan_fwd(c, s):
        st_s[c, :dk, :] = s.astype(BF16)
        return gf_c * s + kv_s[c, :dk, :]

    def scan_bwd(t, s):
        c = nc - 1 - t
        st_s[c, dk:, :] = s.astype(BF16)
        return gb_c * s + kv_s[c, dk:, :]

    s_f = lax.fori_loop(0, nc, scan_fwd, s0f_ref[0, 0])
    s_b = lax.fori_loop(0, nc, scan_bwd, s0b_ref[0, 0])
    if want_states:
        sfo_ref[0, 0] = s_f
        sbo_ref[0, 0] = s_b

    def chunk_outputs(c, carry):
        sl = chunk(c)
        q = q_s[sl, :] if rope else q_ref[0, sl, :]
        k = k_s[sl, :] if rope else k_ref[0, sl, :]
        s = lax.dot_general(q, k, (((1,), (1,)), ((), ())), preferred_element_type=F32)
        o = _dot((s * m_s[...]).astype(BF16), v_ref[0, sl, :])
        qd = jnp.concatenate([q * d_s[0], q * d_s[1]], axis=1)
        o = o + _dot(qd, st_s[c])
        o = o * lax.rsqrt(jnp.mean(o * o, axis=-1, keepdims=True) + EPS)
        o_ref[0, sl, :] = (o * _silu(rg_ref[0, sl, :].astype(F32))).astype(o_ref.dtype)
        return carry

    if want_out:
        lax.fori_loop(0, nc, chunk_outputs, 0, unroll=unroll)


def _retention_call(proj, q_off, decay_logit, s0_f, s0_b, *, rope, want_out, want_states):
    bsz, n, _ = proj.shape
    nh, dk, ch = RET_HEADS, RET_DK, RET_CHUNK
    assert n % ch == 0
    cb = q_off // dk
    col = lambda j: pl.BlockSpec((1, n, dk), functools.partial(lambda b, h, j: (b, 0, cb + j * nh + h), j=j))
    st_spec = pl.BlockSpec((1, 1, dk, dk), lambda b, h: (b, h, 0, 0))
    in_specs = [pl.BlockSpec(memory_space=pltpu.SMEM), col(0), col(1), col(2)]
    args = [decay_logit, proj, proj, proj]
    if want_out:
        in_specs.append(col(3))
        args.append(proj)
    if rope:
        cos, sin = _rope_tables(n)
        in_specs += [_const_spec((n, dk)), _const_spec((n, dk))]
        args += [cos, sin]
    in_specs += [st_spec, st_spec]
    args += [s0_f, s0_b]
    out_specs, out_shape = [], []
    if want_out:
        out_specs.append(pl.BlockSpec((1, n, dk), lambda b, h: (b, 0, h)))
        out_shape.append(jax.ShapeDtypeStruct((bsz, n, RET_WIDTH), BF16))
    if want_states:
        out_specs += [st_spec, st_spec]
        out_shape += [jax.ShapeDtypeStruct((bsz, nh, dk, dk), F32)] * 2
    kern = functools.partial(_retention_kernel, n=n, rope=rope, want_out=want_out, want_states=want_states)
    nc = n // ch
    scratch = [pltpu.VMEM((nc, 2 * dk, dk), F32), pltpu.VMEM((nc, 2 * dk, dk), BF16),
               pltpu.VMEM((ch, ch), F32), pltpu.VMEM((4, ch, dk), BF16)]
    if rope:
        scratch += [pltpu.VMEM((n, dk), BF16), pltpu.VMEM((n, dk), BF16)]
    est = (12 * n * dk * 2 + 2 * n * dk * 4 + nc * 2 * dk * dk * 6 + 16 * ch * ch * 4)
    return pl.pallas_call(
        kern,
        grid=(bsz, nh),
        in_specs=in_specs,
        out_specs=out_specs,
        out_shape=out_shape,
        scratch_shapes=scratch,
        compiler_params=_params(("arbitrary", "arbitrary"), est),
        name="retention",
    )(*args)


def _merge_kernel(uf_ref, up_ref, ur_ref, g0_ref, g1_ref, g2_ref, x_ref, gate_ref,
                  wf_ref, wp_ref, wr_ref, wo_ref, fg_ref, o_ref, *, final):
    m = jax.nn.sigmoid(g0_ref[0].astype(F32)) * _dot(uf_ref[0], wf_ref[...])
    m = m + jax.nn.sigmoid(g1_ref[0].astype(F32)) * _dot(up_ref[0], wp_ref[...])
    m = m + jax.nn.sigmoid(g2_ref[0].astype(F32)) * _dot(ur_ref[0], wr_ref[...])
    y = _dot(m.astype(BF16), wo_ref[...])
    xn = x_ref[0] + gate_ref[0] * y
    if final:
        ms = jnp.mean(xn * xn, axis=-1, keepdims=True)
        xn = xn * lax.rsqrt(ms + EPS) * fg_ref[...]
    o_ref[0] = xn


def _merge_call(u_f, u_p, u_r, proj, x, mod3, mod_row, w_up_f, w_up_p, w_up_r, w_out, layer, final_g, final):
    bsz, n, d = x.shape
    tm = min(MERGE_TM, n)
    row = (lambda b: b) if mod_row is None else (lambda b: mod_row)
    tok = lambda width, cblk: pl.BlockSpec((1, tm, width), lambda b, i: (b, i, cblk))
    gcb = MG_OFF // d
    est = (2 * tm * (F_WIDTH + P_WIDTH + RET_WIDTH) * 2 + 6 * tm * d * 2 + 4 * tm * d * 4
           + (F_WIDTH + P_WIDTH + RET_WIDTH + d) * d * 2 + 4 * tm * d * 4)
    kern = functools.partial(_merge_kernel, final=final)
    return pl.pallas_call(
        kern,
        grid=(bsz, n // tm),
        in_specs=[
            tok(F_WIDTH, 0), tok(P_WIDTH, 0), tok(RET_WIDTH, 0),
            tok(d, gcb), tok(d, gcb + 1), tok(d, gcb + 2),
            tok(d, 0),
            pl.BlockSpec((1, 1, d), lambda b, i: (row(b), 0, 2)),
            _layer_spec(w_up_f.shape, layer), _layer_spec(w_up_p.shape, layer), _layer_spec(w_up_r.shape, layer),
            _layer_spec(w_out.shape, layer),
            _const_spec((1, d)),
        ],
        out_specs=pl.BlockSpec((1, tm, d), lambda b, i: (b, i, 0)),
        out_shape=jax.ShapeDtypeStruct((bsz, n, d), F32),
        compiler_params=_params(("arbitrary", "arbitrary"), est),
        name="merge",
    )(u_f, u_p, u_r, proj, proj, proj, x, mod3, w_up_f, w_up_p, w_up_r, w_out, final_g.reshape(1, d))


def kernel(x, c, ctx, c_ctx, w_ada, b_ada, norm_g, w_in, w_fourier, w_pool, pool_scale, ret_decay_logit,
           w_up_fourier, w_up_pool, w_up_ret, w_out, final_norm_g):
    bsz, n, d = x.shape
    depth = w_ada.shape[0]
    assert d == D_MODEL and bsz + 1 <= MOD_ROWS

    cond_rows = jnp.zeros((MOD_ROWS, d), F32).at[:bsz].set(c).at[bsz].set(c_ctx)
    mod = _ada_call(cond_rows, w_ada, b_ada)

    w_in_b = _wprep_call(w_in)
    w_uf_b, w_upl_b = w_up_fourier.astype(BF16), w_up_pool.astype(BF16)
    w_ur_b, w_out_b = w_up_ret.astype(BF16), w_out.astype(BF16)
    s_zero = jnp.zeros((bsz, RET_HEADS, RET_DK, RET_DK), F32)

    lc = ctx.shape[1]

    def ctx_inproj(l, mod3, col0, ncols):
        p = _inproj_call(ctx.reshape(1, bsz * lc, d), norm_g[l], mod3, bsz, w_in_b, l, col0, ncols)
        return p.reshape(bsz, lc, ncols)

    for l in range(depth):
        last = l == depth - 1
        mod3 = mod[l].reshape(MOD_ROWS, 1, 3 * d)
        if last:
            proj_c = ctx_inproj(l, mod3, R_Q_OFF, MG_OFF - R_Q_OFF)
            s_f, s_b = _retention_call(proj_c, 0, ret_decay_logit[l], s_zero, s_zero,
                                       rope=False, want_out=False, want_states=True)
        else:
            proj_c = ctx_inproj(l, mod3, 0, IN_WIDTH)
            uf_c = _fourier_call(proj_c, w_fourier[l])
            up_c = _pool_call(proj_c, w_pool[l], pool_scale[l])
            ur_c, s_f, s_b = _retention_call(proj_c, R_Q_OFF, ret_decay_logit[l], s_zero, s_zero,
                                             rope=False, want_out=True, want_states=True)
            ctx = _merge_call(uf_c, up_c, ur_c, proj_c, ctx, mod3, bsz, w_uf_b, w_upl_b, w_ur_b,
                              w_out_b, l, final_norm_g, False)
        proj = _inproj_call(x, norm_g[l], mod3, None, w_in_b, l, 0, IN_WIDTH)
        u_f = _fourier_call(proj, w_fourier[l])
        u_p = _pool_call(proj, w_pool[l], pool_scale[l])
        (u_r,) = _retention_call(proj, R_Q_OFF, ret_decay_logit[l], s_f, s_b,
                                 rope=True, want_out=True, want_states=False)
        x = _merge_call(u_f, u_p, u_r, proj, x, mod3, None, w_uf_b, w_upl_b, w_ur_b,
                        w_out_b, l, final_norm_g, last)
    return x
```

```python
import functools
import math

import numpy as np
import jax
import jax.numpy as jnp
from jax import lax
from jax.experimental import pallas as pl
from jax.experimental.pallas import tpu as pltpu

F32 = jnp.float32
BF16 = jnp.bfloat16

D_MODEL = 2048
GRID_W = 64
F_GROUPS = 4
F_WIDTH = D_MODEL // 4
F_GDIM = F_WIDTH // F_GROUPS
POOL_WINDOWS = (2, 4, 8, 16)
P_WIDTH = D_MODEL // 4
P_GDIM = P_WIDTH // len(POOL_WINDOWS)
RET_WIDTH = D_MODEL // 2
RET_HEADS = 8
RET_DK = RET_WIDTH // RET_HEADS
N_BRANCH = 3
ROPE_BASE = 10000.0
EPS = 1e-6

F_X_OFF = 0
F_G_OFF = F_X_OFF + F_WIDTH
P_X_OFF = F_G_OFF + F_WIDTH
P_G_OFF = P_X_OFF + P_WIDTH
R_Q_OFF = P_G_OFF + P_WIDTH
R_K_OFF = R_Q_OFF + RET_WIDTH
R_V_OFF = R_K_OFF + RET_WIDTH
R_G_OFF = R_V_OFF + RET_WIDTH
MG_OFF = R_G_OFF + RET_WIDTH
IN_WIDTH = MG_OFF + N_BRANCH * D_MODEL
MAIN_MG_OFF = R_Q_OFF

V7X_VMEM_BYTES = 64 * 1024 * 1024
VMEM_CAP_BYTES = V7X_VMEM_BYTES - 8 * 1024 * 1024
MOD_ROWS = 8
POOL_HALO = 16
RET_CHUNK = 256
IN_TM, IN_TN = 1024, 1024
MERGE_TM = 512
DFT_TM = 256
DFT_RADIX = 4
POOL_TM = 256


def _vmem_limit(estimate_bytes):
    return int(min(max(estimate_bytes * 5 // 4 + (4 << 20), 32 << 20), VMEM_CAP_BYTES))


def _params(sem, estimate_bytes):
    return pltpu.CompilerParams(dimension_semantics=sem, vmem_limit_bytes=_vmem_limit(estimate_bytes))


def _const_spec(shape):
    nd = len(shape)
    return pl.BlockSpec(shape, lambda *_: (0,) * nd, pipeline_mode=pl.Buffered(1))


def _layer_spec(shape, layer):
    return pl.BlockSpec((None,) + tuple(shape[1:]), lambda *_: (layer, 0, 0), pipeline_mode=pl.Buffered(1))


def _silu(v):
    return v * jax.nn.sigmoid(v)


def _dot(a, b):
    return jnp.dot(a, b, preferred_element_type=F32)


def _ada_kernel(s_ref, w_ref, b_ref, o_ref):
    s = _silu(s_ref[...])
    o_ref[0] = _dot(s.astype(BF16), w_ref[0].astype(BF16)) + b_ref[0]


def _ada_call(cond_rows, w_ada, b_ada):
    depth, d, w3 = w_ada.shape
    tn = 1024
    return pl.pallas_call(
        _ada_kernel,
        grid=(depth, w3 // tn),
        in_specs=[
            pl.BlockSpec((MOD_ROWS, d), lambda l, j: (0, 0)),
            pl.BlockSpec((1, d, tn), lambda l, j: (l, 0, j)),
            pl.BlockSpec((1, 1, tn), lambda l, j: (l, 0, j)),
        ],
        out_specs=pl.BlockSpec((1, MOD_ROWS, tn), lambda l, j: (l, 0, j)),
        out_shape=jax.ShapeDtypeStruct((depth, MOD_ROWS, w3), F32),
        compiler_params=_params(("arbitrary", "arbitrary"), 2 * d * tn * 4 + d * tn * 2),
        name="ada_mod",
    )(cond_rows, w_ada, b_ada.reshape(depth, 1, w3))


def _wprep_kernel(w_ref, o_ref, *, qk_lo, qk_hi):
    j = pl.program_id(1)
    is_qk = jnp.logical_and(j >= qk_lo, j < qk_hi)

    @pl.when(jnp.logical_not(is_qk))
    def _():
        o_ref[0] = w_ref[0].astype(BF16)

    @pl.when(is_qk)
    def _():
        quarter = RET_DK // 4
        for hd in range(w_ref.shape[2] // RET_DK):
            sl = slice(hd * RET_DK, (hd + 1) * RET_DK)
            u = w_ref[0, :, sl]
            lane = lax.broadcasted_iota(jnp.int32, u.shape, 1)
            up = pltpu.roll(u, RET_DK - quarter, 1)
            dn = pltpu.roll(u, quarter, 1)
            v = jnp.where(jnp.logical_and(lane >= quarter, lane < 2 * quarter), up,
                          jnp.where(jnp.logical_and(lane >= 2 * quarter, lane < 3 * quarter), dn, u))
            o_ref[0, :, sl] = v.astype(BF16)


def _wprep_call(w_in):
    depth, d, width = w_in.shape
    tn = IN_TN
    assert R_Q_OFF % tn == 0 and R_V_OFF % tn == 0
    kern = functools.partial(_wprep_kernel, qk_lo=R_Q_OFF // tn, qk_hi=R_V_OFF // tn)
    return pl.pallas_call(
        kern,
        grid=(depth, width // tn),
        in_specs=[pl.BlockSpec((1, d, tn), lambda l, j: (l, 0, j))],
        out_specs=pl.BlockSpec((1, d, tn), lambda l, j: (l, 0, j)),
        out_shape=jax.ShapeDtypeStruct((depth, d, width), BF16),
        compiler_params=_params(("arbitrary", "arbitrary"), 2 * d * tn * 6 + d * tn * 8),
        name="w_prep",
    )(w_in)


def _inproj_kernel(*refs, ret_lo, ret_hi, has_main, has_ret):
    x_ref, g_ref, sc_ref, sh_ref, w_ref = refs[:5]
    outs = list(refs[5:-1])
    h_ref = refs[-1]
    om_ref = outs.pop(0) if has_main else None
    or_ref = outs.pop(0) if has_ret else None
    j = pl.program_id(2)

    @pl.when(j == 0)
    def _():
        x = x_ref[0]
        ms = jnp.mean(x * x, axis=-1, keepdims=True)
        y = x * lax.rsqrt(ms + EPS) * g_ref[...]
        h_ref[...] = (y * (1.0 + sc_ref[0]) + sh_ref[0]).astype(BF16)

    res = _dot(h_ref[...], w_ref[...]).astype(BF16)
    is_ret = jnp.logical_and(j >= ret_lo, j < ret_hi)

    if has_main:
        @pl.when(jnp.logical_not(is_ret))
        def _():
            om_ref[0] = res

    if has_ret:
        @pl.when(is_ret)
        def _():
            for hd in range(res.shape[1] // RET_DK):
                or_ref[0, hd] = res[:, hd * RET_DK:(hd + 1) * RET_DK]


def _inproj_call(x, norm_g, mod3, mod_row, w_bf16, layer, col0, ncols):
    bsz, n, d = x.shape
    tm = min(IN_TM, n)
    tn = IN_TN
    assert tn == RET_WIDTH and R_Q_OFF % tn == 0 and col0 % tn == 0
    cb0, nj = col0 // tn, ncols // tn
    ret_lo = min(max(R_Q_OFF // tn - cb0, 0), nj)
    ret_hi = min(max(MG_OFF // tn - cb0, 0), nj)
    n_ret, n_main = ret_hi - ret_lo, nj - (ret_hi - ret_lo)
    heads = tn // RET_DK
    row = (lambda b: b) if mod_row is None else (lambda b: mod_row)

    def main_tile(j):
        after = jnp.maximum(j - n_ret, max(ret_lo - 1, 0))
        return jnp.where(j < ret_lo, j, jnp.minimum(after, max(n_main - 1, 0)))

    def ret_tile(j):
        return jnp.clip(j - ret_lo, 0, max(n_ret - 1, 0))

    out_specs, out_shape = [], []
    if n_main:
        out_specs.append(pl.BlockSpec((1, tm, tn), lambda b, i, j: (b, i, main_tile(j))))
        out_shape.append(jax.ShapeDtypeStruct((bsz, n, n_main * tn), BF16))
    if n_ret:
        out_specs.append(pl.BlockSpec((1, heads, tm, RET_DK), lambda b, i, j: (b, ret_tile(j), i, 0)))
        out_shape.append(jax.ShapeDtypeStruct((bsz, n_ret * heads, n, RET_DK), BF16))
    kern = functools.partial(_inproj_kernel, ret_lo=ret_lo, ret_hi=ret_hi,
                             has_main=bool(n_main), has_ret=bool(n_ret))
    est = 2 * tm * d * 4 + 2 * d * tn * 2 + 4 * tm * tn * 2 + tm * d * 2 + tm * tn * 6 + tm * d * 4
    outs = pl.pallas_call(
        kern,
        grid=(bsz, n // tm, nj),
        in_specs=[
            pl.BlockSpec((1, tm, d), lambda b, i, j: (b, i, 0)),
            pl.BlockSpec((1, d), lambda b, i, j: (0, 0)),
            pl.BlockSpec((1, 1, d), lambda b, i, j: (row(b), 0, 1)),
            pl.BlockSpec((1, 1, d), lambda b, i, j: (row(b), 0, 0)),
            pl.BlockSpec((None, d, tn), lambda b, i, j: (layer, 0, cb0 + j)),
        ],
        out_specs=out_specs,
        out_shape=out_shape,
        scratch_shapes=[pltpu.VMEM((tm, d), BF16)],
        compiler_params=_params(("arbitrary", "arbitrary", "arbitrary"), est),
        name="in_proj",
    )(x, norm_g.reshape(1, d), mod3, mod3, w_bf16)
    outs = list(outs)
    main = outs.pop(0) if n_main else None
    ret = outs.pop(0) if n_ret else None
    return main, ret


@functools.lru_cache(maxsize=None)
def _dft_tables(n, tm, radix):
    m = n // radix
    k = np.arange(m, dtype=np.int64)[None, :]
    r = np.arange(tm, dtype=np.int64)[:, None]
    ang_b = 2.0 * np.pi * ((r * k) % m) / m
    base = (np.arange(m // tm, dtype=np.int64) * tm)[:, None]
    ang_a = 2.0 * np.pi * ((base * k) % m) / m
    rows = np.arange(m, dtype=np.int64)[None, :, None]
    phase = np.arange(radix, dtype=np.int64)[:, None, None]
    ang_t = np.broadcast_to(2.0 * np.pi * ((rows * phase) % n) / n, (radix, m, F_GDIM))
    c = np.arange(F_GDIM, dtype=np.int64)
    ang_c = 2.0 * np.pi * ((c[:, None] * c[None, :]) % F_GDIM) / F_GDIM
    ortho = 1.0 / math.sqrt(n * F_GDIM)
    f = lambda a: np.asarray(a, dtype=np.float32)
    return (f(np.cos(ang_b)), f(np.sin(ang_b)),
            f(np.cos(ang_a))[:, None, :], f(np.sin(ang_a))[:, None, :],
            f(np.cos(ang_t)), f(np.sin(ang_t)),
            f(np.cos(ang_c) * ortho), f(np.sin(ang_c) * ortho))


def _unit_combine(terms):
    acc = None
    for coef, arr in terms:
        if abs(coef) < 1e-9:
            continue
        if abs(abs(coef) - 1.0) < 1e-9:
            term, neg = arr, coef < 0
        else:
            term, neg = arr * coef, False
        if acc is None:
            acc = -term if neg else term
        else:
            acc = acc - term if neg else acc + term
    return acc


def _fourier_kernel(*refs, bsz, tm, radix):
    cb_ref, sb_ref, ca_ref, sa_ref, twc_ref, tws_ref, cc_ref, sc_ref, wf_ref = refs[:9]
    x_refs = refs[9:9 + bsz * radix]
    fg_ref, o_ref, lhs_ref, ab_ref = refs[9 + bsz * radix:]
    gd = F_GDIM

    @pl.when(pl.program_id(0) == 0)
    def _():
        for g in range(F_GROUPS):
            w = wf_ref[g]
            ab_ref[g, :gd, :] = jnp.dot(cc_ref[...], w, preferred_element_type=F32,
                                        precision=lax.Precision.HIGHEST).astype(BF16)
            ab_ref[g, gd:, :] = (-jnp.dot(sc_ref[...], w, preferred_element_type=F32,
                                          precision=lax.Precision.HIGHEST)).astype(BF16)

    ca, sa = ca_ref[0], sa_ref[0]
    cb, sb = cb_ref[...], sb_ref[...]
    lhs_ref[:tm, :] = (ca * cb - sa * sb).astype(BF16)
    lhs_ref[tm:, :] = (sa * cb + ca * sb).astype(BF16)

    for b in range(bsz):
        ec, es = [], []
        for r in range(radix):
            p = _dot(lhs_ref[...], x_refs[b * radix + r][0])
            ec.append(p[:tm])
            es.append(p[tm:])
        for g in range(F_GROUPS):
            sl = slice(g * gd, (g + 1) * gd)
            tc, ts = [ec[0][:, sl]], [es[0][:, sl]]
            for r in range(1, radix):
                c, s = twc_ref[r], tws_ref[r]
                tc.append(c * ec[r][:, sl] - s * es[r][:, sl])
                ts.append(c * es[r][:, sl] + s * ec[r][:, sl])
            for q in range(radix):
                rot = [(math.cos(2.0 * math.pi * q * r / radix), math.sin(2.0 * math.pi * q * r / radix))
                       for r in range(radix)]
                pc = _unit_combine([(a, tc[r]) for r, (a, _) in enumerate(rot)]
                                   + [(-bb, ts[r]) for r, (_, bb) in enumerate(rot)])
                ps = _unit_combine([(a, ts[r]) for r, (a, _) in enumerate(rot)]
                                   + [(bb, tc[r]) for r, (_, bb) in enumerate(rot)])
                cat = jnp.concatenate([pc, ps], axis=1).astype(BF16)
                y = _dot(cat, ab_ref[g])
                o_ref[b, q, :, sl] = (y * _silu(fg_ref[b, q, :, sl].astype(F32))).astype(o_ref.dtype)


def _fourier_call(proj, w_fourier):
    bsz, n, width = proj.shape
    radix = DFT_RADIX
    m = n // radix
    tm = min(DFT_TM, m)
    cb, sb, ca, sa, twc, tws, cc, sc = _dft_tables(n, tm, radix)
    kern = functools.partial(_fourier_kernel, bsz=bsz, tm=tm, radix=radix)
    phases = proj[:, :, F_X_OFF:F_X_OFF + F_WIDTH].reshape(bsz, m, radix, F_WIDTH).transpose(0, 2, 1, 3)
    x_specs = [pl.BlockSpec((1, None, m, F_WIDTH), functools.partial(lambda i, b, r: (b, r, 0, 0), b=b, r=r),
                            pipeline_mode=pl.Buffered(1))
               for b in range(bsz) for r in range(radix)]
    quarters = proj.reshape(bsz, radix, m, width)
    est = (2 * tm * m * 4 + bsz * n * F_WIDTH * 2 + 2 * tm * m * 2 + 4 * bsz * radix * tm * F_WIDTH * 2
           + 4 * radix * tm * F_GDIM * 4 + (2 * radix + 4) * 2 * tm * F_WIDTH * 4 + 3 * tm * m * 4)
    out = pl.pallas_call(
        kern,
        grid=(m // tm,),
        in_specs=[
            _const_spec((tm, m)), _const_spec((tm, m)),
            pl.BlockSpec((1, 1, m), lambda i: (i, 0, 0)),
            pl.BlockSpec((1, 1, m), lambda i: (i, 0, 0)),
            pl.BlockSpec((radix, tm, F_GDIM), lambda i: (0, i, 0)),
            pl.BlockSpec((radix, tm, F_GDIM), lambda i: (0, i, 0)),
            _const_spec((F_GDIM, F_GDIM)), _const_spec((F_GDIM, F_GDIM)),
            _const_spec((F_GROUPS, F_GDIM, F_GDIM)),
            *x_specs,
            pl.BlockSpec((bsz, radix, tm, F_WIDTH), lambda i: (0, 0, i, F_G_OFF // F_WIDTH)),
        ],
        out_specs=pl.BlockSpec((bsz, radix, tm, F_WIDTH), lambda i: (0, 0, i, 0)),
        out_shape=jax.ShapeDtypeStruct((bsz, radix, m, F_WIDTH), BF16),
        scratch_shapes=[pltpu.VMEM((2 * tm, m), BF16),
                        pltpu.VMEM((F_GROUPS, 2 * F_GDIM, F_GDIM), BF16)],
        compiler_params=_params(("arbitrary",), est),
        name="fourier",
    )(cb, sb, ca, sa, twc, tws, cc, sc, w_fourier, *([phases] * (bsz * radix)), quarters)
    return out.reshape(bsz, n, F_WIDTH)


@functools.lru_cache(maxsize=None)
def _pool_bands(tm):
    r = np.arange(tm)[:, None]
    main, prev, nxt = [], [], []
    for w in POOL_WINDOWS:
        hw = w // 2
        band = lambda d: ((d >= -hw) & (d <= hw - 1)).astype(np.float32)
        main.append(band(np.arange(tm)[None, :] - r))
        prev.append(band(np.arange(POOL_HALO)[None, :] - POOL_HALO - r))
        nxt.append(band(np.arange(POOL_HALO)[None, :] + tm - r))
    return np.stack(main), np.stack(prev), np.stack(nxt)


def _pool_kernel(xm_ref, xp_ref, xn_ref, pg_ref, bm_ref, bp_ref, bn_ref, w_ref, ps_ref, o_ref, *, n, tm):
    i = pl.program_id(1)
    has_prev = jnp.where(i > 0, 1.0, 0.0).astype(F32)
    has_next = jnp.where(i < pl.num_programs(1) - 1, 1.0, 0.0).astype(F32)
    t = i * tm + lax.broadcasted_iota(jnp.int32, (tm, P_GDIM), 0)
    for g, w in enumerate(POOL_WINDOWS):
        sl = slice(g * P_GDIM, (g + 1) * P_GDIM)
        xg = xm_ref[0, :, sl]
        s = _dot(bm_ref[g], xg)
        s = s + has_prev * _dot(bp_ref[g], xp_ref[0, :, sl])
        s = s + has_next * _dot(bn_ref[g], xn_ref[0, :, sl])
        cnt = (jnp.minimum(t + w // 2, n) - jnp.maximum(t - w // 2, 0)).astype(F32)
        pooled = s / cnt - xg.astype(F32)
        y = _dot(pooled.astype(BF16), w_ref[g].astype(BF16)) * ps_ref[:, sl]
        o_ref[0, :, sl] = (y * _silu(pg_ref[0, :, sl].astype(F32))).astype(o_ref.dtype)


def _pool_call(proj, w_pool, pool_scale):
    bsz, n, _ = proj.shape
    tm = min(POOL_TM, n)
    bands = [jnp.asarray(a, dtype=BF16) for a in _pool_bands(tm)]
    hb = tm // POOL_HALO
    nhb = n // POOL_HALO
    xcol = P_X_OFF // P_WIDTH
    kern = functools.partial(_pool_kernel, n=n, tm=tm)
    return pl.pallas_call(
        kern,
        grid=(bsz, n // tm),
        in_specs=[
            pl.BlockSpec((1, tm, P_WIDTH), lambda b, i: (b, i, xcol)),
            pl.BlockSpec((1, POOL_HALO, P_WIDTH), lambda b, i: (b, jnp.maximum(i * hb - 1, 0), xcol)),
            pl.BlockSpec((1, POOL_HALO, P_WIDTH), lambda b, i: (b, jnp.minimum((i + 1) * hb, nhb - 1), xcol)),
            pl.BlockSpec((1, tm, P_WIDTH), lambda b, i: (b, i, P_G_OFF // P_WIDTH)),
            _const_spec(bands[0].shape), _const_spec(bands[1].shape), _const_spec(bands[2].shape),
            _const_spec(w_pool.shape),
            _const_spec((1, P_WIDTH)),
        ],
        out_specs=pl.BlockSpec((1, tm, P_WIDTH), lambda b, i: (b, i, 0)),
        out_shape=jax.ShapeDtypeStruct((bsz, n, P_WIDTH), BF16),
        compiler_params=_params(("arbitrary", "arbitrary"), 16 << 20),
        name="pool",
    )(proj, proj, proj, proj, *bands, w_pool, pool_scale.reshape(1, P_WIDTH))


@functools.lru_cache(maxsize=None)
def _rope_tables(n):
    pos = np.arange(n)
    nf = RET_DK // 4
    inv = ROPE_BASE ** (-np.arange(nf, dtype=np.float64) / nf)
    ang_r = (pos // GRID_W)[:, None] * inv[None, :]
    ang_c = (pos % GRID_W)[:, None] * inv[None, :]
    cos = np.concatenate([np.cos(ang_r), np.cos(ang_c)] * 2, axis=1)
    sin = np.concatenate([-np.sin(ang_r), -np.sin(ang_c), np.sin(ang_r), np.sin(ang_c)], axis=1)
    return np.asarray(cos, np.float32), np.asarray(sin, np.float32)


def _log_sigmoid(v):
    return jnp.minimum(v, 0.0) - jnp.log1p(jnp.exp(-jnp.abs(v)))


def _retention_kernel(*refs, n, rope, want_out, want_states):
    it = iter(refs)
    dl_ref = next(it)
    q_ref, k_ref, v_ref = next(it), next(it), next(it)
    rg_ref = next(it) if want_out else None
    cos_ref, sin_ref = (next(it), next(it)) if rope else (None, None)
    s0f_ref, s0b_ref = next(it), next(it)
    o_ref = next(it) if want_out else None
    sfo_ref, sbo_ref = (next(it), next(it)) if want_states else (None, None)
    kv_s, st_s, m_s, d_s = next(it), next(it), next(it), next(it)
    q_s, k_s = (next(it), next(it)) if rope else (None, None)

    ch, dk = RET_CHUNK, RET_DK
    nc = n // ch
    unroll = min(2, nc)
    h = pl.program_id(1)
    scale = dk ** -0.5

    def log_gamma(direction, shape):
        return _log_sigmoid(jnp.full(shape, dl_ref[direction, h], F32))

    lg_f, lg_b = log_gamma(0, (ch, dk)), log_gamma(1, (ch, dk))
    row = lax.broadcasted_iota(jnp.int32, (ch, dk), 0).astype(F32)
    d_s[0] = (jnp.exp(lg_f * (row + 1.0)) * scale).astype(BF16)
    d_s[1] = (jnp.exp(lg_b * (ch - row)) * scale).astype(BF16)
    d_s[2] = jnp.exp(lg_f * (ch - 1.0 - row)).astype(BF16)
    d_s[3] = jnp.exp(lg_b * row).astype(BF16)
    gf_c = jnp.exp(log_gamma(0, (dk, dk)) * ch)
    gb_c = jnp.exp(log_gamma(1, (dk, dk)) * ch)
    if want_out:
        diff = (lax.broadcasted_iota(jnp.int32, (ch, ch), 0)
                - lax.broadcasted_iota(jnp.int32, (ch, ch), 1)).astype(F32)
        lf, lb = log_gamma(0, (ch, ch)), log_gamma(1, (ch, ch))
        m_s[...] = scale * (jnp.where(diff >= 0, jnp.exp(lf * jnp.maximum(diff, 0.0)), 0.0)
                            + jnp.where(diff <= 0, jnp.exp(lb * jnp.maximum(-diff, 0.0)), 0.0))

    def chunk(c):
        return pl.ds(c * ch if isinstance(c, int) else pl.multiple_of(c * ch, ch), ch)

    def rotate(u, cos, sin):
        u = u.astype(F32)
        return (u * cos + pltpu.roll(u, dk // 2, 1) * sin).astype(BF16)

    def chunk_states(c, carry):
        sl = chunk(c)
        k = k_ref[0, sl, :]
        if rope:
            cos, sin = cos_ref[sl, :], sin_ref[sl, :]
            k = rotate(k, cos, sin)
            k_s[sl, :] = k
            if want_out:
                q_s[sl, :] = rotate(q_ref[0, sl, :], cos, sin)
        kd = jnp.concatenate([k * d_s[2], k * d_s[3]], axis=1)
        kv_s[c] = lax.dot_general(kd, v_ref[0, sl, :], (((0,), (0,)), ((), ())), preferred_element_type=F32)
        return carry

    lax.fori_loop(0, nc, chunk_states, 0, unroll=unroll)

    def scan_fwd(c, s):
        st_s[c, :dk, :] = s.astype(BF16)
        return gf_c * s + kv_s[c, :dk, :]

    def scan_bwd(t, s):
        c = nc - 1 - t
        st_s[c, dk:, :] = s.astype(BF16)
        return gb_c * s + kv_s[c, dk:, :]

    s_f = lax.fori_loop(0, nc, scan_fwd, s0f_ref[0, 0])
    s_b = lax.fori_loop(0, nc, scan_bwd, s0b_ref[0, 0])
    if want_states:
        sfo_ref[0, 0] = s_f
        sbo_ref[0, 0] = s_b

    def chunk_matmuls(c):
        sl = chunk(c)
        q = q_s[sl, :] if rope else q_ref[0, sl, :]
        k = k_s[sl, :] if rope else k_ref[0, sl, :]
        s = lax.dot_general(q, k, (((1,), (1,)), ((), ())), preferred_element_type=F32)
        o = _dot((s * m_s[...]).astype(BF16), v_ref[0, sl, :])
        qd = jnp.concatenate([q * d_s[0], q * d_s[1]], axis=1)
        return o + _dot(qd, st_s[c])

    def chunk_finish(c, o):
        sl = chunk(c)
        o = o * lax.rsqrt(jnp.mean(o * o, axis=-1, keepdims=True) + EPS)
        o_ref[0, sl, :] = (o * _silu(rg_ref[0, sl, :].astype(F32))).astype(o_ref.dtype)

    def chunk_outputs(c, o_prev):
        o = chunk_matmuls(c)
        chunk_finish(c - 1, o_prev)
        return o

    if want_out:
        o_last = lax.fori_loop(1, nc, chunk_outputs, chunk_matmuls(0))
        chunk_finish(nc - 1, o_last)


def _retention_call(proj, bsz, decay_logit, s0_f, s0_b, *, rope, want_out, want_states):
    flat = proj.shape[0] != bsz
    n = proj.shape[2] // bsz if flat else proj.shape[2]
    nh, dk, ch = RET_HEADS, RET_DK, RET_CHUNK
    assert n % ch == 0
    where = (lambda b, hd: (0, hd, b, 0)) if flat else (lambda b, hd: (b, hd, 0, 0))
    col = lambda j: pl.BlockSpec((1, None, n, dk), functools.partial(lambda b, h, j: where(b, j * nh + h), j=j))
    st_spec = pl.BlockSpec((1, 1, dk, dk), lambda b, h: (b, h, 0, 0))
    in_specs = [pl.BlockSpec(memory_space=pltpu.SMEM), col(0), col(1), col(2)]
    args = [decay_logit, proj, proj, proj]
    if want_out:
        in_specs.append(col(3))
        args.append(proj)
    if rope:
        cos, sin = _rope_tables(n)
        in_specs += [_const_spec((n, dk)), _const_spec((n, dk))]
        args += [cos, sin]
    in_specs += [st_spec, st_spec]
    args += [s0_f, s0_b]
    out_specs, out_shape = [], []
    if want_out:
        out_specs.append(pl.BlockSpec((1, n, dk), lambda b, h: (b, 0, h)))
        out_shape.append(jax.ShapeDtypeStruct((bsz, n, RET_WIDTH), BF16))
    if want_states:
        out_specs += [st_spec, st_spec]
        out_shape += [jax.ShapeDtypeStruct((bsz, nh, dk, dk), F32)] * 2
    kern = functools.partial(_retention_kernel, n=n, rope=rope, want_out=want_out, want_states=want_states)
    nc = n // ch
    scratch = [pltpu.VMEM((nc, 2 * dk, dk), F32), pltpu.VMEM((nc, 2 * dk, dk), BF16),
               pltpu.VMEM((ch, ch), F32), pltpu.VMEM((4, ch, dk), BF16)]
    if rope:
        scratch += [pltpu.VMEM((n, dk), BF16), pltpu.VMEM((n, dk), BF16)]
    est = (12 * n * dk * 2 + 2 * n * dk * 4 + nc * 2 * dk * dk * 6 + 16 * ch * ch * 4)
    return pl.pallas_call(
        kern,
        grid=(bsz, nh),
        in_specs=in_specs,
        out_specs=out_specs,
        out_shape=out_shape,
        scratch_shapes=scratch,
        compiler_params=_params(("arbitrary", "arbitrary"), est),
        name="retention",
    )(*args)


def _merge_kernel(uf_ref, up_ref, ur_ref, g0_ref, g1_ref, g2_ref, x_ref, gate_ref,
                  wf_ref, wp_ref, wr_ref, wo_ref, fg_ref, o_ref, *, final):
    m = jax.nn.sigmoid(g0_ref[0].astype(F32)) * _dot(uf_ref[0], wf_ref[...])
    m = m + jax.nn.sigmoid(g1_ref[0].astype(F32)) * _dot(up_ref[0], wp_ref[...])
    m = m + jax.nn.sigmoid(g2_ref[0].astype(F32)) * _dot(ur_ref[0], wr_ref[...])
    y = _dot(m.astype(BF16), wo_ref[...])
    xn = x_ref[0] + gate_ref[0] * y
    if final:
        ms = jnp.mean(xn * xn, axis=-1, keepdims=True)
        xn = xn * lax.rsqrt(ms + EPS) * fg_ref[...]
    o_ref[0] = xn


def _merge_call(u_f, u_p, u_r, proj, x, mod3, mod_row, w_up_f, w_up_p, w_up_r, w_out, layer, final_g, final):
    bsz, n, d = x.shape
    tm = min(MERGE_TM, n)
    row = (lambda b: b) if mod_row is None else (lambda b: mod_row)
    tok = lambda width, cblk: pl.BlockSpec((1, tm, width), lambda b, i: (b, i, cblk))
    gcb = MAIN_MG_OFF // d
    est = (2 * tm * (F_WIDTH + P_WIDTH + RET_WIDTH) * 2 + 6 * tm * d * 2 + 4 * tm * d * 4
           + (F_WIDTH + P_WIDTH + RET_WIDTH + d) * d * 2 + 4 * tm * d * 4)
    kern = functools.partial(_merge_kernel, final=final)
    return pl.pallas_call(
        kern,
        grid=(bsz, n // tm),
        in_specs=[
            tok(F_WIDTH, 0), tok(P_WIDTH, 0), tok(RET_WIDTH, 0),
            tok(d, gcb), tok(d, gcb + 1), tok(d, gcb + 2),
            tok(d, 0),
            pl.BlockSpec((1, 1, d), lambda b, i: (row(b), 0, 2)),
            _layer_spec(w_up_f.shape, layer), _layer_spec(w_up_p.shape, layer), _layer_spec(w_up_r.shape, layer),
            _layer_spec(w_out.shape, layer),
            _const_spec((1, d)),
        ],
        out_specs=pl.BlockSpec((1, tm, d), lambda b, i: (b, i, 0)),
        out_shape=jax.ShapeDtypeStruct((bsz, n, d), F32),
        compiler_params=_params(("arbitrary", "arbitrary"), est),
        name="merge",
    )(u_f, u_p, u_r, proj, proj, proj, x, mod3, w_up_f, w_up_p, w_up_r, w_out, final_g.reshape(1, d))


def kernel(x, c, ctx, c_ctx, w_ada, b_ada, norm_g, w_in, w_fourier, w_pool, pool_scale, ret_decay_logit,
           w_up_fourier, w_up_pool, w_up_ret, w_out, final_norm_g):
    bsz, n, d = x.shape
    depth = w_ada.shape[0]
    assert d == D_MODEL and bsz + 1 <= MOD_ROWS

    cond_rows = jnp.zeros((MOD_ROWS, d), F32).at[:bsz].set(c).at[bsz].set(c_ctx)
    mod = _ada_call(cond_rows, w_ada, b_ada)

    w_in_b = _wprep_call(w_in)
    w_uf_b, w_upl_b = w_up_fourier.astype(BF16), w_up_pool.astype(BF16)
    w_ur_b, w_out_b = w_up_ret.astype(BF16), w_out.astype(BF16)
    s_zero = jnp.zeros((bsz, RET_HEADS, RET_DK, RET_DK), F32)

    lc = ctx.shape[1]

    def ctx_inproj(l, mod3, col0, ncols):
        main, ret = _inproj_call(ctx.reshape(1, bsz * lc, d), norm_g[l], mod3, bsz, w_in_b, l, col0, ncols)
        if main is not None:
            main = main.reshape(bsz, lc, main.shape[-1])
        return main, ret

    for l in range(depth):
        last = l == depth - 1
        mod3 = mod[l].reshape(MOD_ROWS, 1, 3 * d)
        if last:
            _, ret_c = ctx_inproj(l, mod3, R_Q_OFF, MG_OFF - R_Q_OFF)
            s_f, s_b = _retention_call(ret_c, bsz, ret_decay_logit[l], s_zero, s_zero,
                                       rope=False, want_out=False, want_states=True)
        else:
            proj_c, ret_c = ctx_inproj(l, mod3, 0, IN_WIDTH)
            uf_c = _fourier_call(proj_c, w_fourier[l])
            up_c = _pool_call(proj_c, w_pool[l], pool_scale[l])
            ur_c, s_f, s_b = _retention_call(ret_c, bsz, ret_decay_logit[l], s_zero, s_zero,
                                             rope=False, want_out=True, want_states=True)
            ctx = _merge_call(uf_c, up_c, ur_c, proj_c, ctx, mod3, bsz, w_uf_b, w_upl_b, w_ur_b,
                              w_out_b, l, final_norm_g, False)
        proj, ret = _inproj_call(x, norm_g[l], mod3, None, w_in_b, l, 0, IN_WIDTH)
        u_f = _fourier_call(proj, w_fourier[l])
        u_p = _pool_call(proj, w_pool[l], pool_scale[l])
        (u_r,) = _retention_call(ret, bsz, ret_decay_logit[l], s_f, s_b,
                                 rope=True, want_out=True, want_states=False)
        x = _merge_call(u_f, u_p, u_r, proj, x, mod3, None, w_uf_b, w_upl_b, w_ur_b,
                        w_out_b, l, final_norm_g, last)
    return x
```

```python
import functools
import math

import numpy as np
import jax
import jax.numpy as jnp
from jax import lax
from jax.experimental import pallas as pl
from jax.experimental.pallas import tpu as pltpu

F32 = jnp.float32
BF16 = jnp.bfloat16

D_MODEL = 2048
GRID_W = 64
F_GROUPS = 4
F_WIDTH = D_MODEL // 4
F_GDIM = F_WIDTH // F_GROUPS
POOL_WINDOWS = (2, 4, 8, 16)
P_WIDTH = D_MODEL // 4
P_GDIM = P_WIDTH // len(POOL_WINDOWS)
RET_WIDTH = D_MODEL // 2
RET_HEADS = 8
RET_DK = RET_WIDTH // RET_HEADS
N_BRANCH = 3
ROPE_BASE = 10000.0
EPS = 1e-6

F_X_OFF = 0
F_G_OFF = F_X_OFF + F_WIDTH
P_X_OFF = F_G_OFF + F_WIDTH
P_G_OFF = P_X_OFF + P_WIDTH
R_Q_OFF = P_G_OFF + P_WIDTH
R_K_OFF = R_Q_OFF + RET_WIDTH
R_V_OFF = R_K_OFF + RET_WIDTH
R_G_OFF = R_V_OFF + RET_WIDTH
MG_OFF = R_G_OFF + RET_WIDTH
IN_WIDTH = MG_OFF + N_BRANCH * D_MODEL

V7X_VMEM_BYTES = 64 * 1024 * 1024
VMEM_CAP_BYTES = V7X_VMEM_BYTES - 8 * 1024 * 1024
MOD_ROWS = 8
POOL_HALO = 16
RET_CHUNK = 256
IN_TM, IN_TN = 1024, 1024
MERGE_TM = 512
DFT_TM = 256
DFT_RADIX = 4
POOL_TM = 512


def _vmem_limit(estimate_bytes):
    return int(min(max(estimate_bytes * 5 // 4 + (4 << 20), 32 << 20), VMEM_CAP_BYTES))


def _params(sem, estimate_bytes):
    return pltpu.CompilerParams(dimension_semantics=sem, vmem_limit_bytes=_vmem_limit(estimate_bytes))


def _const_spec(shape):
    nd = len(shape)
    return pl.BlockSpec(shape, lambda *_: (0,) * nd, pipeline_mode=pl.Buffered(1))


def _layer_spec(shape, layer):
    return pl.BlockSpec((None,) + tuple(shape[1:]), lambda *_: (layer, 0, 0), pipeline_mode=pl.Buffered(1))


def _silu(v):
    return v * jax.nn.sigmoid(v)


def _dot(a, b):
    return jnp.dot(a, b, preferred_element_type=F32)


def _ada_kernel(s_ref, w_ref, b_ref, o_ref):
    s = _silu(s_ref[...])
    o_ref[0] = _dot(s.astype(BF16), w_ref[0].astype(BF16)) + b_ref[0]


def _ada_call(cond_rows, w_ada, b_ada):
    depth, d, w3 = w_ada.shape
    tn = 1024
    return pl.pallas_call(
        _ada_kernel,
        grid=(depth, w3 // tn),
        in_specs=[
            pl.BlockSpec((MOD_ROWS, d), lambda l, j: (0, 0)),
            pl.BlockSpec((1, d, tn), lambda l, j: (l, 0, j)),
            pl.BlockSpec((1, 1, tn), lambda l, j: (l, 0, j)),
        ],
        out_specs=pl.BlockSpec((1, MOD_ROWS, tn), lambda l, j: (l, 0, j)),
        out_shape=jax.ShapeDtypeStruct((depth, MOD_ROWS, w3), F32),
        compiler_params=_params(("arbitrary", "arbitrary"), 2 * d * tn * 4 + d * tn * 2),
        name="ada_mod",
    )(cond_rows, w_ada, b_ada.reshape(depth, 1, w3))


def _wprep_kernel(w_ref, o_ref, *, qk_lo, qk_hi):
    j = pl.program_id(1)
    is_qk = jnp.logical_and(j >= qk_lo, j < qk_hi)

    @pl.when(jnp.logical_not(is_qk))
    def _():
        o_ref[0] = w_ref[0].astype(BF16)

    @pl.when(is_qk)
    def _():
        quarter = RET_DK // 4
        for hd in range(w_ref.shape[2] // RET_DK):
            sl = slice(hd * RET_DK, (hd + 1) * RET_DK)
            u = w_ref[0, :, sl]
            lane = lax.broadcasted_iota(jnp.int32, u.shape, 1)
            up = pltpu.roll(u, RET_DK - quarter, 1)
            dn = pltpu.roll(u, quarter, 1)
            v = jnp.where(jnp.logical_and(lane >= quarter, lane < 2 * quarter), up,
                          jnp.where(jnp.logical_and(lane >= 2 * quarter, lane < 3 * quarter), dn, u))
            o_ref[0, :, sl] = v.astype(BF16)


def _wprep_call(w_in):
    depth, d, width = w_in.shape
    tn = IN_TN
    assert R_Q_OFF % tn == 0 and R_V_OFF % tn == 0
    kern = functools.partial(_wprep_kernel, qk_lo=R_Q_OFF // tn, qk_hi=R_V_OFF // tn)
    return pl.pallas_call(
        kern,
        grid=(depth, width // tn),
        in_specs=[pl.BlockSpec((1, d, tn), lambda l, j: (l, 0, j))],
        out_specs=pl.BlockSpec((1, d, tn), lambda l, j: (l, 0, j)),
        out_shape=jax.ShapeDtypeStruct((depth, d, width), BF16),
        compiler_params=_params(("arbitrary", "arbitrary"), 2 * d * tn * 6 + d * tn * 8),
        name="w_prep",
    )(w_in)


def _inproj_kernel(x_ref, g_ref, sc_ref, sh_ref, w_ref, o_ref, h_ref):
    @pl.when(pl.program_id(2) == 0)
    def _():
        x = x_ref[0]
        ms = jnp.mean(x * x, axis=-1, keepdims=True)
        y = x * lax.rsqrt(ms + EPS) * g_ref[...]
        h_ref[...] = (y * (1.0 + sc_ref[0]) + sh_ref[0]).astype(BF16)

    o_ref[0] = _dot(h_ref[...], w_ref[...]).astype(o_ref.dtype)


def _inproj_call(x, norm_g, mod3, mod_row, w_bf16, layer, col0, ncols):
    bsz, n, d = x.shape
    tm = min(IN_TM, n)
    tn = IN_TN
    cb0 = col0 // tn
    row = (lambda b: b) if mod_row is None else (lambda b: mod_row)
    est = 2 * tm * d * 4 + 2 * d * tn * 2 + 2 * tm * tn * 2 + tm * d * 2 + tm * tn * 4 + tm * d * 4
    return pl.pallas_call(
        _inproj_kernel,
        grid=(bsz, n // tm, ncols // tn),
        in_specs=[
            pl.BlockSpec((1, tm, d), lambda b, i, j: (b, i, 0)),
            pl.BlockSpec((1, d), lambda b, i, j: (0, 0)),
            pl.BlockSpec((1, 1, d), lambda b, i, j: (row(b), 0, 1)),
            pl.BlockSpec((1, 1, d), lambda b, i, j: (row(b), 0, 0)),
            pl.BlockSpec((None, d, tn), lambda b, i, j: (layer, 0, cb0 + j)),
        ],
        out_specs=pl.BlockSpec((1, tm, tn), lambda b, i, j: (b, i, j)),
        out_shape=jax.ShapeDtypeStruct((bsz, n, ncols), BF16),
        scratch_shapes=[pltpu.VMEM((tm, d), BF16)],
        compiler_params=_params(("arbitrary", "arbitrary", "arbitrary"), est),
        name="in_proj",
    )(x, norm_g.reshape(1, d), mod3, mod3, w_bf16)


@functools.lru_cache(maxsize=None)
def _dft_tables(n, tm, radix):
    m = n // radix
    k = np.arange(m, dtype=np.int64)[None, :]
    r = np.arange(tm, dtype=np.int64)[:, None]
    ang_b = 2.0 * np.pi * ((r * k) % m) / m
    base = (np.arange(m // tm, dtype=np.int64) * tm)[:, None]
    ang_a = 2.0 * np.pi * ((base * k) % m) / m
    rows = np.arange(m, dtype=np.int64)[None, :, None]
    phase = np.arange(radix, dtype=np.int64)[:, None, None]
    ang_t = np.broadcast_to(2.0 * np.pi * ((rows * phase) % n) / n, (radix, m, F_GDIM))
    c = np.arange(F_GDIM, dtype=np.int64)
    ang_c = 2.0 * np.pi * ((c[:, None] * c[None, :]) % F_GDIM) / F_GDIM
    ortho = 1.0 / math.sqrt(n * F_GDIM)
    f = lambda a: np.asarray(a, dtype=np.float32)
    return (f(np.cos(ang_b)), f(np.sin(ang_b)),
            f(np.cos(ang_a))[:, None, :], f(np.sin(ang_a))[:, None, :],
            f(np.cos(ang_t)), f(np.sin(ang_t)),
            f(np.cos(ang_c) * ortho), f(np.sin(ang_c) * ortho))


def _unit_combine(terms):
    acc = None
    for coef, arr in terms:
        if abs(coef) < 1e-9:
            continue
        if abs(abs(coef) - 1.0) < 1e-9:
            term, neg = arr, coef < 0
        else:
            term, neg = arr * coef, False
        if acc is None:
            acc = -term if neg else term
        else:
            acc = acc - term if neg else acc + term
    return acc


def _fourier_kernel(*refs, bsz, tm, radix):
    cb_ref, sb_ref, ca_ref, sa_ref, twc_ref, tws_ref, cc_ref, sc_ref, wf_ref = refs[:9]
    x_refs = refs[9:9 + bsz * radix]
    fg_ref, o_ref, lhs_ref, ab_ref = refs[9 + bsz * radix:]
    gd = F_GDIM

    @pl.when(pl.program_id(0) == 0)
    def _():
        for g in range(F_GROUPS):
            w = wf_ref[g]
            ab_ref[g, :gd, :] = jnp.dot(cc_ref[...], w, preferred_element_type=F32,
                                        precision=lax.Precision.HIGHEST).astype(BF16)
            ab_ref[g, gd:, :] = (-jnp.dot(sc_ref[...], w, preferred_element_type=F32,
                                          precision=lax.Precision.HIGHEST)).astype(BF16)

    ca, sa = ca_ref[0], sa_ref[0]
    cb, sb = cb_ref[...], sb_ref[...]
    lhs_ref[:tm, :] = (ca * cb - sa * sb).astype(BF16)
    lhs_ref[tm:, :] = (sa * cb + ca * sb).astype(BF16)

    for b in range(bsz):
        ec, es = [], []
        for r in range(radix):
            p = _dot(lhs_ref[...], x_refs[b * radix + r][0])
            ec.append(p[:tm])
            es.append(p[tm:])
        for g in range(F_GROUPS):
            sl = slice(g * gd, (g + 1) * gd)
            tc, ts = [ec[0][:, sl]], [es[0][:, sl]]
            for r in range(1, radix):
                c, s = twc_ref[r], tws_ref[r]
                tc.append(c * ec[r][:, sl] - s * es[r][:, sl])
                ts.append(c * es[r][:, sl] + s * ec[r][:, sl])
            for q in range(radix):
                rot = [(math.cos(2.0 * math.pi * q * r / radix), math.sin(2.0 * math.pi * q * r / radix))
                       for r in range(radix)]
                pc = _unit_combine([(a, tc[r]) for r, (a, _) in enumerate(rot)]
                                   + [(-bb, ts[r]) for r, (_, bb) in enumerate(rot)])
                ps = _unit_combine([(a, ts[r]) for r, (a, _) in enumerate(rot)]
                                   + [(bb, tc[r]) for r, (_, bb) in enumerate(rot)])
                cat = jnp.concatenate([pc, ps], axis=1).astype(BF16)
                y = _dot(cat, ab_ref[g])
                o_ref[b, q, :, sl] = (y * _silu(fg_ref[b, q, :, sl].astype(F32))).astype(o_ref.dtype)


def _fourier_call(proj, w_fourier):
    bsz, n, width = proj.shape
    radix = DFT_RADIX
    m = n // radix
    tm = min(DFT_TM, m)
    cb, sb, ca, sa, twc, tws, cc, sc = _dft_tables(n, tm, radix)
    kern = functools.partial(_fourier_kernel, bsz=bsz, tm=tm, radix=radix)
    phases = proj[:, :, F_X_OFF:F_X_OFF + F_WIDTH].reshape(bsz, m, radix, F_WIDTH).transpose(0, 2, 1, 3)
    x_specs = [pl.BlockSpec((1, None, m, F_WIDTH), functools.partial(lambda i, b, r: (b, r, 0, 0), b=b, r=r),
                            pipeline_mode=pl.Buffered(1))
               for b in range(bsz) for r in range(radix)]
    quarters = proj.reshape(bsz, radix, m, width)
    est = (2 * tm * m * 4 + bsz * n * F_WIDTH * 2 + 2 * tm * m * 2 + 4 * bsz * radix * tm * F_WIDTH * 2
           + 4 * radix * tm * F_GDIM * 4 + (2 * radix + 4) * 2 * tm * F_WIDTH * 4 + 3 * tm * m * 4)
    out = pl.pallas_call(
        kern,
        grid=(m // tm,),
        in_specs=[
            _const_spec((tm, m)), _const_spec((tm, m)),
            pl.BlockSpec((1, 1, m), lambda i: (i, 0, 0)),
            pl.BlockSpec((1, 1, m), lambda i: (i, 0, 0)),
            pl.BlockSpec((radix, tm, F_GDIM), lambda i: (0, i, 0)),
            pl.BlockSpec((radix, tm, F_GDIM), lambda i: (0, i, 0)),
            _const_spec((F_GDIM, F_GDIM)), _const_spec((F_GDIM, F_GDIM)),
            _const_spec((F_GROUPS, F_GDIM, F_GDIM)),
            *x_specs,
            pl.BlockSpec((bsz, radix, tm, F_WIDTH), lambda i: (0, 0, i, F_G_OFF // F_WIDTH)),
        ],
        out_specs=pl.BlockSpec((bsz, radix, tm, F_WIDTH), lambda i: (0, 0, i, 0)),
        out_shape=jax.ShapeDtypeStruct((bsz, radix, m, F_WIDTH), BF16),
        scratch_shapes=[pltpu.VMEM((2 * tm, m), BF16),
                        pltpu.VMEM((F_GROUPS, 2 * F_GDIM, F_GDIM), BF16)],
        compiler_params=_params(("arbitrary",), est),
        name="fourier",
    )(cb, sb, ca, sa, twc, tws, cc, sc, w_fourier, *([phases] * (bsz * radix)), quarters)
    return out.reshape(bsz, n, F_WIDTH)


@functools.lru_cache(maxsize=None)
def _pool_bands(tm):
    r = np.arange(tm)[:, None]
    main, prev, nxt = [], [], []
    for w in POOL_WINDOWS:
        hw = w // 2
        band = lambda d: ((d >= -hw) & (d <= hw - 1)).astype(np.float32)
        main.append(band(np.arange(tm)[None, :] - r))
        prev.append(band(np.arange(POOL_HALO)[None, :] - POOL_HALO - r))
        nxt.append(band(np.arange(POOL_HALO)[None, :] + tm - r))
    return np.stack(main), np.stack(prev), np.stack(nxt)


def _pool_kernel(xm_ref, xp_ref, xn_ref, pg_ref, bm_ref, bp_ref, bn_ref, w_ref, ps_ref, o_ref, *, n, tm):
    i = pl.program_id(1)
    has_prev = jnp.where(i > 0, 1.0, 0.0).astype(F32)
    has_next = jnp.where(i < pl.num_programs(1) - 1, 1.0, 0.0).astype(F32)
    t = i * tm + lax.broadcasted_iota(jnp.int32, (tm, P_GDIM), 0)
    for g, w in enumerate(POOL_WINDOWS):
        sl = slice(g * P_GDIM, (g + 1) * P_GDIM)
        xg = xm_ref[0, :, sl]
        s = _dot(bm_ref[g], xg)
        s = s + has_prev * _dot(bp_ref[g], xp_ref[0, :, sl])
        s = s + has_next * _dot(bn_ref[g], xn_ref[0, :, sl])
        cnt = (jnp.minimum(t + w // 2, n) - jnp.maximum(t - w // 2, 0)).astype(F32)
        pooled = s / cnt - xg.astype(F32)
        y = _dot(pooled.astype(BF16), w_ref[g].astype(BF16)) * ps_ref[:, sl]
        o_ref[0, :, sl] = (y * _silu(pg_ref[0, :, sl].astype(F32))).astype(o_ref.dtype)


def _pool_call(proj, w_pool, pool_scale):
    bsz, n, _ = proj.shape
    tm = min(POOL_TM, n)
    bands = [jnp.asarray(a, dtype=BF16) for a in _pool_bands(tm)]
    hb = tm // POOL_HALO
    nhb = n // POOL_HALO
    xcol = P_X_OFF // P_WIDTH
    kern = functools.partial(_pool_kernel, n=n, tm=tm)
    return pl.pallas_call(
        kern,
        grid=(bsz, n // tm),
        in_specs=[
            pl.BlockSpec((1, tm, P_WIDTH), lambda b, i: (b, i, xcol)),
            pl.BlockSpec((1, POOL_HALO, P_WIDTH), lambda b, i: (b, jnp.maximum(i * hb - 1, 0), xcol)),
            pl.BlockSpec((1, POOL_HALO, P_WIDTH), lambda b, i: (b, jnp.minimum((i + 1) * hb, nhb - 1), xcol)),
            pl.BlockSpec((1, tm, P_WIDTH), lambda b, i: (b, i, P_G_OFF // P_WIDTH)),
            _const_spec(bands[0].shape), _const_spec(bands[1].shape), _const_spec(bands[2].shape),
            _const_spec(w_pool.shape),
            _const_spec((1, P_WIDTH)),
        ],
        out_specs=pl.BlockSpec((1, tm, P_WIDTH), lambda b, i: (b, i, 0)),
        out_shape=jax.ShapeDtypeStruct((bsz, n, P_WIDTH), BF16),
        compiler_params=_params(("arbitrary", "arbitrary"), 16 << 20),
        name="pool",
    )(proj, proj, proj, proj, *bands, w_pool, pool_scale.reshape(1, P_WIDTH))


@functools.lru_cache(maxsize=None)
def _rope_tables(n):
    pos = np.arange(n)
    nf = RET_DK // 4
    inv = ROPE_BASE ** (-np.arange(nf, dtype=np.float64) / nf)
    ang_r = (pos // GRID_W)[:, None] * inv[None, :]
    ang_c = (pos % GRID_W)[:, None] * inv[None, :]
    cos = np.concatenate([np.cos(ang_r), np.cos(ang_c)] * 2, axis=1)
    sin = np.concatenate([-np.sin(ang_r), -np.sin(ang_c), np.sin(ang_r), np.sin(ang_c)], axis=1)
    return np.asarray(cos, np.float32), np.asarray(sin, np.float32)


def _log_sigmoid(v):
    return jnp.minimum(v, 0.0) - jnp.log1p(jnp.exp(-jnp.abs(v)))


def _retention_kernel(*refs, n, rope, want_out, want_states):
    it = iter(refs)
    dl_ref = next(it)
    q_ref, k_ref, v_ref = next(it), next(it), next(it)
    rg_ref = next(it) if want_out else None
    cos_ref, sin_ref = (next(it), next(it)) if rope else (None, None)
    s0f_ref, s0b_ref = next(it), next(it)
    o_ref = next(it) if want_out else None
    sfo_ref, sbo_ref = (next(it), next(it)) if want_states else (None, None)
    kv_s, st_s, m_s, d_s = next(it), next(it), next(it), next(it)
    q_s, k_s = (next(it), next(it)) if rope else (None, None)

    ch, dk = RET_CHUNK, RET_DK
    nc = n // ch
    unroll = min(2, nc)
    h = pl.program_id(1)
    scale = dk ** -0.5

    def log_gamma(direction, shape):
        return _log_sigmoid(jnp.full(shape, dl_ref[direction, h], F32))

    lg_f, lg_b = log_gamma(0, (ch, dk)), log_gamma(1, (ch, dk))
    row = lax.broadcasted_iota(jnp.int32, (ch, dk), 0).astype(F32)
    d_s[0] = (jnp.exp(lg_f * (row + 1.0)) * scale).astype(BF16)
    d_s[1] = (jnp.exp(lg_b * (ch - row)) * scale).astype(BF16)
    d_s[2] = jnp.exp(lg_f * (ch - 1.0 - row)).astype(BF16)
    d_s[3] = jnp.exp(lg_b * row).astype(BF16)
    gf_c = jnp.exp(log_gamma(0, (dk, dk)) * ch)
    gb_c = jnp.exp(log_gamma(1, (dk, dk)) * ch)
    if want_out:
        diff = (lax.broadcasted_iota(jnp.int32, (ch, ch), 0)
                - lax.broadcasted_iota(jnp.int32, (ch, ch), 1)).astype(F32)
        lf, lb = log_gamma(0, (ch, ch)), log_gamma(1, (ch, ch))
        m_s[...] = scale * (jnp.where(diff >= 0, jnp.exp(lf * jnp.maximum(diff, 0.0)), 0.0)
                            + jnp.where(diff <= 0, jnp.exp(lb * jnp.maximum(-diff, 0.0)), 0.0))

    def chunk(c):
        return pl.ds(c * ch if isinstance(c, int) else pl.multiple_of(c * ch, ch), ch)

    def rotate(u, cos, sin):
        u = u.astype(F32)
        return (u * cos + pltpu.roll(u, dk // 2, 1) * sin).astype(BF16)

    def chunk_states(c, carry):
        sl = chunk(c)
        k = k_ref[0, sl, :]
        if rope:
            cos, sin = cos_ref[sl, :], sin_ref[sl, :]
            k = rotate(k, cos, sin)
            k_s[sl, :] = k
            if want_out:
                q_s[sl, :] = rotate(q_ref[0, sl, :], cos, sin)
        kd = jnp.concatenate([k * d_s[2], k * d_s[3]], axis=1)
        kv_s[c] = lax.dot_general(kd, v_ref[0, sl, :], (((0,), (0,)), ((), ())), preferred_element_type=F32)
        return carry

    lax.fori_loop(0, nc, chunk_states, 0, unroll=unroll)

    def scan_fwd(c, s):
        st_s[c, :dk, :] = s.astype(BF16)
        return gf_c * s + kv_s[c, :dk, :]

    def scan_bwd(t, s):
        c = nc - 1 - t
        st_s[c, dk:, :] = s.astype(BF16)
        return gb_c * s + kv_s[c, dk:, :]

    s_f = lax.fori_loop(0, nc, scan_fwd, s0f_ref[0, 0])
    s_b = lax.fori_loop(0, nc, scan_bwd, s0b_ref[0, 0])
    if want_states:
        sfo_ref[0, 0] = s_f
        sbo_ref[0, 0] = s_b

    def chunk_matmuls(c):
        sl = chunk(c)
        q = q_s[sl, :] if rope else q_ref[0, sl, :]
        k = k_s[sl, :] if rope else k_ref[0, sl, :]
        s = lax.dot_general(q, k, (((1,), (1,)), ((), ())), preferred_element_type=F32)
        o = _dot((s * m_s[...]).astype(BF16), v_ref[0, sl, :])
        qd = jnp.concatenate([q * d_s[0], q * d_s[1]], axis=1)
        return o + _dot(qd, st_s[c])

    def chunk_finish(c, o):
        sl = chunk(c)
        o = o * lax.rsqrt(jnp.mean(o * o, axis=-1, keepdims=True) + EPS)
        o_ref[0, sl, :] = (o * _silu(rg_ref[0, sl, :].astype(F32))).astype(o_ref.dtype)

    if want_out:
        pair = 2 if nc % 2 == 0 else 1

        def chunk_outputs(t, prev):
            cur = tuple(chunk_matmuls(t * pair + u) for u in range(pair))
            for u in range(pair):
                chunk_finish((t - 1) * pair + u, prev[u])
            return cur

        first = tuple(chunk_matmuls(u) for u in range(pair))
        last = lax.fori_loop(1, nc // pair, chunk_outputs, first)
        for u in range(pair):
            chunk_finish(nc - pair + u, last[u])


def _retention_call(proj, q_off, decay_logit, s0_f, s0_b, *, rope, want_out, want_states):
    bsz, n, _ = proj.shape
    nh, dk, ch = RET_HEADS, RET_DK, RET_CHUNK
    assert n % ch == 0
    cb = q_off // dk
    col = lambda j: pl.BlockSpec((1, n, dk), functools.partial(lambda b, h, j: (b, 0, cb + j * nh + h), j=j))
    st_spec = pl.BlockSpec((1, 1, dk, dk), lambda b, h: (b, h, 0, 0))
    in_specs = [pl.BlockSpec(memory_space=pltpu.SMEM), col(0), col(1), col(2)]
    args = [decay_logit, proj, proj, proj]
    if want_out:
        in_specs.append(col(3))
        args.append(proj)
    if rope:
        cos, sin = _rope_tables(n)
        in_specs += [_const_spec((n, dk)), _const_spec((n, dk))]
        args += [cos, sin]
    in_specs += [st_spec, st_spec]
    args += [s0_f, s0_b]
    out_specs, out_shape = [], []
    if want_out:
        out_specs.append(pl.BlockSpec((1, n, dk), lambda b, h: (b, 0, h)))
        out_shape.append(jax.ShapeDtypeStruct((bsz, n, RET_WIDTH), BF16))
    if want_states:
        out_specs += [st_spec, st_spec]
        out_shape += [jax.ShapeDtypeStruct((bsz, nh, dk, dk), F32)] * 2
    kern = functools.partial(_retention_kernel, n=n, rope=rope, want_out=want_out, want_states=want_states)
    nc = n // ch
    scratch = [pltpu.VMEM((nc, 2 * dk, dk), F32), pltpu.VMEM((nc, 2 * dk, dk), BF16),
               pltpu.VMEM((ch, ch), F32), pltpu.VMEM((4, ch, dk), BF16)]
    if rope:
        scratch += [pltpu.VMEM((n, dk), BF16), pltpu.VMEM((n, dk), BF16)]
    est = (12 * n * dk * 2 + 2 * n * dk * 4 + nc * 2 * dk * dk * 6 + 16 * ch * ch * 4)
    return pl.pallas_call(
        kern,
        grid=(bsz, nh),
        in_specs=in_specs,
        out_specs=out_specs,
        out_shape=out_shape,
        scratch_shapes=scratch,
        compiler_params=_params(("arbitrary", "arbitrary"), est),
        name="retention",
    )(*args)


def _merge_kernel(uf_ref, up_ref, ur_ref, g0_ref, g1_ref, g2_ref, x_ref, gate_ref,
                  wf_ref, wp_ref, wr_ref, wo_ref, fg_ref, o_ref, *, final):
    m = jax.nn.sigmoid(g0_ref[0].astype(F32)) * _dot(uf_ref[0], wf_ref[...])
    m = m + jax.nn.sigmoid(g1_ref[0].astype(F32)) * _dot(up_ref[0], wp_ref[...])
    m = m + jax.nn.sigmoid(g2_ref[0].astype(F32)) * _dot(ur_ref[0], wr_ref[...])
    y = _dot(m.astype(BF16), wo_ref[...])
    xn = x_ref[0] + gate_ref[0] * y
    if final:
        ms = jnp.mean(xn * xn, axis=-1, keepdims=True)
        xn = xn * lax.rsqrt(ms + EPS) * fg_ref[...]
    o_ref[0] = xn


def _merge_call(u_f, u_p, u_r, proj, x, mod3, mod_row, w_up_f, w_up_p, w_up_r, w_out, layer, final_g, final):
    bsz, n, d = x.shape
    tm = min(MERGE_TM, n)
    row = (lambda b: b) if mod_row is None else (lambda b: mod_row)
    tok = lambda width, cblk: pl.BlockSpec((1, tm, width), lambda b, i: (b, i, cblk))
    gcb = MG_OFF // d
    est = (2 * tm * (F_WIDTH + P_WIDTH + RET_WIDTH) * 2 + 6 * tm * d * 2 + 4 * tm * d * 4
           + (F_WIDTH + P_WIDTH + RET_WIDTH + d) * d * 2 + 4 * tm * d * 4)
    kern = functools.partial(_merge_kernel, final=final)
    return pl.pallas_call(
        kern,
        grid=(bsz, n // tm),
        in_specs=[
            tok(F_WIDTH, 0), tok(P_WIDTH, 0), tok(RET_WIDTH, 0),
            tok(d, gcb), tok(d, gcb + 1), tok(d, gcb + 2),
            tok(d, 0),
            pl.BlockSpec((1, 1, d), lambda b, i: (row(b), 0, 2)),
            _layer_spec(w_up_f.shape, layer), _layer_spec(w_up_p.shape, layer), _layer_spec(w_up_r.shape, layer),
            _layer_spec(w_out.shape, layer),
            _const_spec((1, d)),
        ],
        out_specs=pl.BlockSpec((1, tm, d), lambda b, i: (b, i, 0)),
        out_shape=jax.ShapeDtypeStruct((bsz, n, d), F32),
        compiler_params=_params(("arbitrary", "arbitrary"), est),
        name="merge",
    )(u_f, u_p, u_r, proj, proj, proj, x, mod3, w_up_f, w_up_p, w_up_r, w_out, final_g.reshape(1, d))


def kernel(x, c, ctx, c_ctx, w_ada, b_ada, norm_g, w_in, w_fourier, w_pool, pool_scale, ret_decay_logit,
           w_up_fourier, w_up_pool, w_up_ret, w_out, final_norm_g):
    bsz, n, d = x.shape
    depth = w_ada.shape[0]
    assert d == D_MODEL and bsz + 1 <= MOD_ROWS

    cond_rows = jnp.zeros((MOD_ROWS, d), F32).at[:bsz].set(c).at[bsz].set(c_ctx)
    mod = _ada_call(cond_rows, w_ada, b_ada)

    w_in_b = _wprep_call(w_in)
    w_uf_b, w_upl_b = w_up_fourier.astype(BF16), w_up_pool.astype(BF16)
    w_ur_b, w_out_b = w_up_ret.astype(BF16), w_out.astype(BF16)
    s_zero = jnp.zeros((bsz, RET_HEADS, RET_DK, RET_DK), F32)

    lc = ctx.shape[1]

    def ctx_inproj(l, mod3, col0, ncols):
        p = _inproj_call(ctx.reshape(1, bsz * lc, d), norm_g[l], mod3, bsz, w_in_b, l, col0, ncols)
        return p.reshape(bsz, lc, ncols)

    for l in range(depth):
        last = l == depth - 1
        mod3 = mod[l].reshape(MOD_ROWS, 1, 3 * d)
        if last:
            proj_c = ctx_inproj(l, mod3, R_Q_OFF, MG_OFF - R_Q_OFF)
            s_f, s_b = _retention_call(proj_c, 0, ret_decay_logit[l], s_zero, s_zero,
                                       rope=False, want_out=False, want_states=True)
        else:
            proj_c = ctx_inproj(l, mod3, 0, IN_WIDTH)
            uf_c = _fourier_call(proj_c, w_fourier[l])
            up_c = _pool_call(proj_c, w_pool[l], pool_scale[l])
            ur_c, s_f, s_b = _retention_call(proj_c, R_Q_OFF, ret_decay_logit[l], s_zero, s_zero,
                                             rope=False, want_out=True, want_states=True)
            ctx = _merge_call(uf_c, up_c, ur_c, proj_c, ctx, mod3, bsz, w_uf_b, w_upl_b, w_ur_b,
                              w_out_b, l, final_norm_g, False)
        proj = _inproj_call(x, norm_g[l], mod3, None, w_in_b, l, 0, IN_WIDTH)
        u_f = _fourier_call(proj, w_fourier[l])
        u_p = _pool_call(proj, w_pool[l], pool_scale[l])
        (u_r,) = _retention_call(proj, R_Q_OFF, ret_decay_logit[l], s_f, s_b,
                                 rope=True, want_out=True, want_states=False)
        x = _merge_call(u_f, u_p, u_r, proj, x, mod3, None, w_uf_b, w_upl_b, w_ur_b,
                        w_out_b, l, final_norm_g, last)
    return x
```

```python
import functools
import math

import numpy as np
import jax
import jax.numpy as jnp
from jax import lax
from jax.experimental import pallas as pl
from jax.experimental.pallas import tpu as pltpu

F32 = jnp.float32
BF16 = jnp.bfloat16

D_MODEL = 2048
GRID_W = 64
F_GROUPS = 4
F_WIDTH = D_MODEL // 4
F_GDIM = F_WIDTH // F_GROUPS
POOL_WINDOWS = (2, 4, 8, 16)
P_WIDTH = D_MODEL // 4
P_GDIM = P_WIDTH // len(POOL_WINDOWS)
RET_WIDTH = D_MODEL // 2
RET_HEADS = 8
RET_DK = RET_WIDTH // RET_HEADS
N_BRANCH = 3
ROPE_BASE = 10000.0
EPS = 1e-6

F_X_OFF = 0
F_G_OFF = F_X_OFF + F_WIDTH
P_X_OFF = F_G_OFF + F_WIDTH
P_G_OFF = P_X_OFF + P_WIDTH
R_Q_OFF = P_G_OFF + P_WIDTH
R_K_OFF = R_Q_OFF + RET_WIDTH
R_V_OFF = R_K_OFF + RET_WIDTH
R_G_OFF = R_V_OFF + RET_WIDTH
MG_OFF = R_G_OFF + RET_WIDTH
IN_WIDTH = MG_OFF + N_BRANCH * D_MODEL

V7X_VMEM_BYTES = 64 * 1024 * 1024
VMEM_CAP_BYTES = V7X_VMEM_BYTES - 8 * 1024 * 1024
MOD_ROWS = 8
POOL_HALO = 16
RET_CHUNK = 256
IN_TM, IN_TN = 1024, 1024
IN_TN_WIDE = 1536
NORM_ROWS = 16
MERGE_TM = 512
DFT_TM = 256
DFT_RADIX = 4
POOL_TM = 512


def _vmem_limit(estimate_bytes):
    return int(min(max(estimate_bytes * 5 // 4 + (4 << 20), 32 << 20), VMEM_CAP_BYTES))


def _params(sem, estimate_bytes):
    return pltpu.CompilerParams(dimension_semantics=sem, vmem_limit_bytes=_vmem_limit(estimate_bytes))


def _const_spec(shape):
    nd = len(shape)
    return pl.BlockSpec(shape, lambda *_: (0,) * nd, pipeline_mode=pl.Buffered(1))


def _layer_spec(shape, layer):
    return pl.BlockSpec((None,) + tuple(shape[1:]), lambda *_: (layer, 0, 0), pipeline_mode=pl.Buffered(1))


def _silu(v):
    return v * jax.nn.sigmoid(v)


def _dot(a, b):
    return jnp.dot(a, b, preferred_element_type=F32)


def _ada_kernel(s_ref, w_ref, b_ref, o_ref):
    s = _silu(s_ref[...])
    o_ref[0] = _dot(s.astype(BF16), w_ref[0].astype(BF16)) + b_ref[0]


def _ada_call(cond_rows, w_ada, b_ada):
    depth, d, w3 = w_ada.shape
    tn = 1024
    return pl.pallas_call(
        _ada_kernel,
        grid=(depth, w3 // tn),
        in_specs=[
            pl.BlockSpec((MOD_ROWS, d), lambda l, j: (0, 0)),
            pl.BlockSpec((1, d, tn), lambda l, j: (l, 0, j)),
            pl.BlockSpec((1, 1, tn), lambda l, j: (l, 0, j)),
        ],
        out_specs=pl.BlockSpec((1, MOD_ROWS, tn), lambda l, j: (l, 0, j)),
        out_shape=jax.ShapeDtypeStruct((depth, MOD_ROWS, w3), F32),
        compiler_params=_params(("arbitrary", "arbitrary"), 2 * d * tn * 4 + d * tn * 2),
        name="ada_mod",
    )(cond_rows, w_ada, b_ada.reshape(depth, 1, w3))


def _wprep_kernel(w_ref, o_ref, *, qk_lo, qk_hi):
    j = pl.program_id(1)
    is_qk = jnp.logical_and(j >= qk_lo, j < qk_hi)

    @pl.when(jnp.logical_not(is_qk))
    def _():
        o_ref[0] = w_ref[0].astype(BF16)

    @pl.when(is_qk)
    def _():
        quarter = RET_DK // 4
        for hd in range(w_ref.shape[2] // RET_DK):
            sl = slice(hd * RET_DK, (hd + 1) * RET_DK)
            u = w_ref[0, :, sl]
            lane = lax.broadcasted_iota(jnp.int32, u.shape, 1)
            up = pltpu.roll(u, RET_DK - quarter, 1)
            dn = pltpu.roll(u, quarter, 1)
            v = jnp.where(jnp.logical_and(lane >= quarter, lane < 2 * quarter), up,
                          jnp.where(jnp.logical_and(lane >= 2 * quarter, lane < 3 * quarter), dn, u))
            o_ref[0, :, sl] = v.astype(BF16)


def _wprep_call(w_in):
    depth, d, width = w_in.shape
    tn = IN_TN
    assert R_Q_OFF % tn == 0 and R_V_OFF % tn == 0
    kern = functools.partial(_wprep_kernel, qk_lo=R_Q_OFF // tn, qk_hi=R_V_OFF // tn)
    return pl.pallas_call(
        kern,
        grid=(depth, width // tn),
        in_specs=[pl.BlockSpec((1, d, tn), lambda l, j: (l, 0, j))],
        out_specs=pl.BlockSpec((1, d, tn), lambda l, j: (l, 0, j)),
        out_shape=jax.ShapeDtypeStruct((depth, d, width), BF16),
        compiler_params=_params(("arbitrary", "arbitrary"), 2 * d * tn * 6 + d * tn * 8),
        name="w_prep",
    )(w_in)


def _inproj_kernel(x_ref, g_ref, sc_ref, sh_ref, w_ref, o_ref, h_ref, gain_ref, shift_ref):
    @pl.when(pl.program_id(2) == 0)
    def _():
        rows = NORM_ROWS
        d = x_ref.shape[2]
        gain_ref[...] = jnp.broadcast_to(g_ref[...] * (1.0 + sc_ref[0]), (rows, d))
        shift_ref[...] = jnp.broadcast_to(sh_ref[0], (rows, d))

        def norm_rows(r, carry):
            sl = pl.ds(pl.multiple_of(r * rows, rows), rows)
            x = x_ref[0, sl, :]
            ms = jnp.mean(x * x, axis=-1, keepdims=True)
            h_ref[sl, :] = (x * lax.rsqrt(ms + EPS) * gain_ref[...] + shift_ref[...]).astype(BF16)
            return carry

        lax.fori_loop(0, x_ref.shape[1] // rows, norm_rows, 0, unroll=4)

    o_ref[0] = _dot(h_ref[...], w_ref[...]).astype(o_ref.dtype)


def _inproj_call(x, norm_g, mod3, mod_row, w_bf16, layer, col0, ncols):
    bsz, n, d = x.shape
    tm = min(IN_TM, n)
    tn = IN_TN_WIDE if (col0 % IN_TN_WIDE == 0 and ncols % IN_TN_WIDE == 0) else IN_TN
    cb0 = col0 // tn
    row = (lambda b: b) if mod_row is None else (lambda b: mod_row)
    est = (2 * tm * d * 4 + 2 * d * tn * 2 + 2 * tm * tn * 2 + tm * d * 2 + tm * tn * 4 + 2 * NORM_ROWS * d * 4
           + 8 * NORM_ROWS * d * 4)
    return pl.pallas_call(
        _inproj_kernel,
        grid=(bsz, n // tm, ncols // tn),
        in_specs=[
            pl.BlockSpec((1, tm, d), lambda b, i, j: (b, i, 0)),
            pl.BlockSpec((1, d), lambda b, i, j: (0, 0)),
            pl.BlockSpec((1, 1, d), lambda b, i, j: (row(b), 0, 1)),
            pl.BlockSpec((1, 1, d), lambda b, i, j: (row(b), 0, 0)),
            pl.BlockSpec((None, d, tn), lambda b, i, j: (layer, 0, cb0 + j)),
        ],
        out_specs=pl.BlockSpec((1, tm, tn), lambda b, i, j: (b, i, j)),
        out_shape=jax.ShapeDtypeStruct((bsz, n, ncols), BF16),
        scratch_shapes=[pltpu.VMEM((tm, d), BF16), pltpu.VMEM((NORM_ROWS, d), F32),
                        pltpu.VMEM((NORM_ROWS, d), F32)],
        compiler_params=_params(("arbitrary", "arbitrary", "arbitrary"), est),
        name="in_proj",
    )(x, norm_g.reshape(1, d), mod3, mod3, w_bf16)


@functools.lru_cache(maxsize=None)
def _dft_tables(n, tm, radix):
    m = n // radix
    k = np.arange(m, dtype=np.int64)[None, :]
    r = np.arange(tm, dtype=np.int64)[:, None]
    ang_b = 2.0 * np.pi * ((r * k) % m) / m
    base = (np.arange(m // tm, dtype=np.int64) * tm)[:, None]
    ang_a = 2.0 * np.pi * ((base * k) % m) / m
    rows = np.arange(m, dtype=np.int64)[None, :, None]
    phase = np.arange(radix, dtype=np.int64)[:, None, None]
    ang_t = np.broadcast_to(2.0 * np.pi * ((rows * phase) % n) / n, (radix, m, F_GDIM))
    c = np.arange(F_GDIM, dtype=np.int64)
    ang_c = 2.0 * np.pi * ((c[:, None] * c[None, :]) % F_GDIM) / F_GDIM
    ortho = 1.0 / math.sqrt(n * F_GDIM)
    f = lambda a: np.asarray(a, dtype=np.float32)
    return (f(np.cos(ang_b)), f(np.sin(ang_b)),
            f(np.cos(ang_a))[:, None, :], f(np.sin(ang_a))[:, None, :],
            f(np.cos(ang_t)), f(np.sin(ang_t)),
            f(np.cos(ang_c) * ortho), f(np.sin(ang_c) * ortho))


def _unit_combine(terms):
    acc = None
    for coef, arr in terms:
        if abs(coef) < 1e-9:
            continue
        if abs(abs(coef) - 1.0) < 1e-9:
            term, neg = arr, coef < 0
        else:
            term, neg = arr * coef, False
        if acc is None:
            acc = -term if neg else term
        else:
            acc = acc - term if neg else acc + term
    return acc


def _fourier_kernel(*refs, bsz, tm, radix):
    cb_ref, sb_ref, ca_ref, sa_ref, twc_ref, tws_ref, cc_ref, sc_ref, wf_ref = refs[:9]
    x_refs = refs[9:9 + bsz * radix]
    fg_ref, o_ref, lhs_ref, ab_ref = refs[9 + bsz * radix:]
    gd = F_GDIM

    @pl.when(pl.program_id(0) == 0)
    def _():
        for g in range(F_GROUPS):
            w = wf_ref[g]
            ab_ref[g, :gd, :] = jnp.dot(cc_ref[...], w, preferred_element_type=F32,
                                        precision=lax.Precision.HIGHEST).astype(BF16)
            ab_ref[g, gd:, :] = (-jnp.dot(sc_ref[...], w, preferred_element_type=F32,
                                          precision=lax.Precision.HIGHEST)).astype(BF16)

    ca, sa = ca_ref[0], sa_ref[0]
    cb, sb = cb_ref[...], sb_ref[...]
    lhs_ref[:tm, :] = (ca * cb - sa * sb).astype(BF16)
    lhs_ref[tm:, :] = (sa * cb + ca * sb).astype(BF16)

    for b in range(bsz):
        ec, es = [], []
        for r in range(radix):
            p = _dot(lhs_ref[...], x_refs[b * radix + r][0])
            ec.append(p[:tm])
            es.append(p[tm:])
        for g in range(F_GROUPS):
            sl = slice(g * gd, (g + 1) * gd)
            tc, ts = [ec[0][:, sl]], [es[0][:, sl]]
            for r in range(1, radix):
                c, s = twc_ref[r], tws_ref[r]
                tc.append(c * ec[r][:, sl] - s * es[r][:, sl])
                ts.append(c * es[r][:, sl] + s * ec[r][:, sl])
            for q in range(radix):
                rot = [(math.cos(2.0 * math.pi * q * r / radix), math.sin(2.0 * math.pi * q * r / radix))
                       for r in range(radix)]
                pc = _unit_combine([(a, tc[r]) for r, (a, _) in enumerate(rot)]
                                   + [(-bb, ts[r]) for r, (_, bb) in enumerate(rot)])
                ps = _unit_combine([(a, ts[r]) for r, (a, _) in enumerate(rot)]
                                   + [(bb, tc[r]) for r, (_, bb) in enumerate(rot)])
                cat = jnp.concatenate([pc, ps], axis=1).astype(BF16)
                y = _dot(cat, ab_ref[g])
                o_ref[b, q, :, sl] = (y * _silu(fg_ref[b, q, :, sl].astype(F32))).astype(o_ref.dtype)


def _fourier_call(proj, w_fourier):
    bsz, n, width = proj.shape
    radix = DFT_RADIX
    m = n // radix
    tm = min(DFT_TM, m)
    cb, sb, ca, sa, twc, tws, cc, sc = _dft_tables(n, tm, radix)
    kern = functools.partial(_fourier_kernel, bsz=bsz, tm=tm, radix=radix)
    phases = proj[:, :, F_X_OFF:F_X_OFF + F_WIDTH].reshape(bsz, m, radix, F_WIDTH).transpose(0, 2, 1, 3)
    x_specs = [pl.BlockSpec((1, None, m, F_WIDTH), functools.partial(lambda i, b, r: (b, r, 0, 0), b=b, r=r),
                            pipeline_mode=pl.Buffered(1))
               for b in range(bsz) for r in range(radix)]
    quarters = proj.reshape(bsz, radix, m, width)
    est = (2 * tm * m * 4 + bsz * n * F_WIDTH * 2 + 2 * tm * m * 2 + 4 * bsz * radix * tm * F_WIDTH * 2
           + 4 * radix * tm * F_GDIM * 4 + (2 * radix + 4) * 2 * tm * F_WIDTH * 4 + 3 * tm * m * 4)
    out = pl.pallas_call(
        kern,
        grid=(m // tm,),
        in_specs=[
            _const_spec((tm, m)), _const_spec((tm, m)),
            pl.BlockSpec((1, 1, m), lambda i: (i, 0, 0)),
            pl.BlockSpec((1, 1, m), lambda i: (i, 0, 0)),
            pl.BlockSpec((radix, tm, F_GDIM), lambda i: (0, i, 0)),
            pl.BlockSpec((radix, tm, F_GDIM), lambda i: (0, i, 0)),
            _const_spec((F_GDIM, F_GDIM)), _const_spec((F_GDIM, F_GDIM)),
            _const_spec((F_GROUPS, F_GDIM, F_GDIM)),
            *x_specs,
            pl.BlockSpec((bsz, radix, tm, F_WIDTH), lambda i: (0, 0, i, F_G_OFF // F_WIDTH)),
        ],
        out_specs=pl.BlockSpec((bsz, radix, tm, F_WIDTH), lambda i: (0, 0, i, 0)),
        out_shape=jax.ShapeDtypeStruct((bsz, radix, m, F_WIDTH), BF16),
        scratch_shapes=[pltpu.VMEM((2 * tm, m), BF16),
                        pltpu.VMEM((F_GROUPS, 2 * F_GDIM, F_GDIM), BF16)],
        compiler_params=_params(("arbitrary",), est),
        name="fourier",
    )(cb, sb, ca, sa, twc, tws, cc, sc, w_fourier, *([phases] * (bsz * radix)), quarters)
    return out.reshape(bsz, n, F_WIDTH)


@functools.lru_cache(maxsize=None)
def _pool_bands(tm):
    r = np.arange(tm)[:, None]
    main, prev, nxt = [], [], []
    for w in POOL_WINDOWS:
        hw = w // 2
        band = lambda d: ((d >= -hw) & (d <= hw - 1)).astype(np.float32)
        main.append(band(np.arange(tm)[None, :] - r))
        prev.append(band(np.arange(POOL_HALO)[None, :] - POOL_HALO - r))
        nxt.append(band(np.arange(POOL_HALO)[None, :] + tm - r))
    return np.stack(main), np.stack(prev), np.stack(nxt)


def _pool_kernel(xm_ref, xp_ref, xn_ref, pg_ref, bm_ref, bp_ref, bn_ref, w_ref, ps_ref, o_ref, *, n, tm):
    i = pl.program_id(1)
    has_prev = jnp.where(i > 0, 1.0, 0.0).astype(F32)
    has_next = jnp.where(i < pl.num_programs(1) - 1, 1.0, 0.0).astype(F32)
    t = i * tm + lax.broadcasted_iota(jnp.int32, (tm, P_GDIM), 0)
    for g, w in enumerate(POOL_WINDOWS):
        sl = slice(g * P_GDIM, (g + 1) * P_GDIM)
        xg = xm_ref[0, :, sl]
        s = _dot(bm_ref[g], xg)
        s = s + has_prev * _dot(bp_ref[g], xp_ref[0, :, sl])
        s = s + has_next * _dot(bn_ref[g], xn_ref[0, :, sl])
        cnt = (jnp.minimum(t + w // 2, n) - jnp.maximum(t - w // 2, 0)).astype(F32)
        pooled = s / cnt - xg.astype(F32)
        y = _dot(pooled.astype(BF16), w_ref[g].astype(BF16)) * ps_ref[:, sl]
        o_ref[0, :, sl] = (y * _silu(pg_ref[0, :, sl].astype(F32))).astype(o_ref.dtype)


def _pool_call(proj, w_pool, pool_scale):
    bsz, n, _ = proj.shape
    tm = min(POOL_TM, n)
    bands = [jnp.asarray(a, dtype=BF16) for a in _pool_bands(tm)]
    hb = tm // POOL_HALO
    nhb = n // POOL_HALO
    xcol = P_X_OFF // P_WIDTH
    kern = functools.partial(_pool_kernel, n=n, tm=tm)
    return pl.pallas_call(
        kern,
        grid=(bsz, n // tm),
        in_specs=[
            pl.BlockSpec((1, tm, P_WIDTH), lambda b, i: (b, i, xcol)),
            pl.BlockSpec((1, POOL_HALO, P_WIDTH), lambda b, i: (b, jnp.maximum(i * hb - 1, 0), xcol)),
            pl.BlockSpec((1, POOL_HALO, P_WIDTH), lambda b, i: (b, jnp.minimum((i + 1) * hb, nhb - 1), xcol)),
            pl.BlockSpec((1, tm, P_WIDTH), lambda b, i: (b, i, P_G_OFF // P_WIDTH)),
            _const_spec(bands[0].shape), _const_spec(bands[1].shape), _const_spec(bands[2].shape),
            _const_spec(w_pool.shape),
            _const_spec((1, P_WIDTH)),
        ],
        out_specs=pl.BlockSpec((1, tm, P_WIDTH), lambda b, i: (b, i, 0)),
        out_shape=jax.ShapeDtypeStruct((bsz, n, P_WIDTH), BF16),
        compiler_params=_params(("arbitrary", "arbitrary"), 16 << 20),
        name="pool",
    )(proj, proj, proj, proj, *bands, w_pool, pool_scale.reshape(1, P_WIDTH))


@functools.lru_cache(maxsize=None)
def _rope_tables(n):
    pos = np.arange(n)
    nf = RET_DK // 4
    inv = ROPE_BASE ** (-np.arange(nf, dtype=np.float64) / nf)
    ang_r = (pos // GRID_W)[:, None] * inv[None, :]
    ang_c = (pos % GRID_W)[:, None] * inv[None, :]
    cos = np.concatenate([np.cos(ang_r), np.cos(ang_c)] * 2, axis=1)
    sin = np.concatenate([-np.sin(ang_r), -np.sin(ang_c), np.sin(ang_r), np.sin(ang_c)], axis=1)
    return np.asarray(cos, np.float32), np.asarray(sin, np.float32)


def _log_sigmoid(v):
    return jnp.minimum(v, 0.0) - jnp.log1p(jnp.exp(-jnp.abs(v)))


def _retention_kernel(*refs, n, rope, want_out, want_states):
    it = iter(refs)
    dl_ref = next(it)
    q_ref, k_ref, v_ref = next(it), next(it), next(it)
    rg_ref = next(it) if want_out else None
    cos_ref, sin_ref = (next(it), next(it)) if rope else (None, None)
    s0f_ref, s0b_ref = next(it), next(it)
    o_ref = next(it) if want_out else None
    sfo_ref, sbo_ref = (next(it), next(it)) if want_states else (None, None)
    kv_s, st_s, m_s, d_s = next(it), next(it), next(it), next(it)
    q_s, k_s = (next(it), next(it)) if rope else (None, None)

    ch, dk = RET_CHUNK, RET_DK
    nc = n // ch
    unroll = min(2, nc)
    h = pl.program_id(1)
    scale = dk ** -0.5

    def log_gamma(direction, shape):
        return _log_sigmoid(jnp.full(shape, dl_ref[direction, h], F32))

    lg_f, lg_b = log_gamma(0, (ch, dk)), log_gamma(1, (ch, dk))
    row = lax.broadcasted_iota(jnp.int32, (ch, dk), 0).astype(F32)
    d_s[0] = (jnp.exp(lg_f * (row + 1.0)) * scale).astype(BF16)
    d_s[1] = (jnp.exp(lg_b * (ch - row)) * scale).astype(BF16)
    d_s[2] = jnp.exp(lg_f * (ch - 1.0 - row)).astype(BF16)
    d_s[3] = jnp.exp(lg_b * row).astype(BF16)
    gf_c = jnp.exp(log_gamma(0, (dk, dk)) * ch)
    gb_c = jnp.exp(log_gamma(1, (dk, dk)) * ch)
    if want_out:
        diff = (lax.broadcasted_iota(jnp.int32, (ch, ch), 0)
                - lax.broadcasted_iota(jnp.int32, (ch, ch), 1)).astype(F32)
        lf, lb = log_gamma(0, (ch, ch)), log_gamma(1, (ch, ch))
        m_s[...] = scale * (jnp.where(diff >= 0, jnp.exp(lf * jnp.maximum(diff, 0.0)), 0.0)
                            + jnp.where(diff <= 0, jnp.exp(lb * jnp.maximum(-diff, 0.0)), 0.0))

    def chunk(c):
        return pl.ds(c * ch if isinstance(c, int) else pl.multiple_of(c * ch, ch), ch)

    def rotate(u, cos, sin):
        u = u.astype(F32)
        return (u * cos + pltpu.roll(u, dk // 2, 1) * sin).astype(BF16)

    def chunk_states(c, carry):
        sl = chunk(c)
        k = k_ref[0, sl, :]
        if rope:
            cos, sin = cos_ref[sl, :], sin_ref[sl, :]
            k = rotate(k, cos, sin)
            k_s[sl, :] = k
            if want_out:
                q_s[sl, :] = rotate(q_ref[0, sl, :], cos, sin)
        kd = jnp.concatenate([k * d_s[2], k * d_s[3]], axis=1)
        kv_s[c] = lax.dot_general(kd, v_ref[0, sl, :], (((0,), (0,)), ((), ())), preferred_element_type=F32)
        return carry

    lax.fori_loop(0, nc, chunk_states, 0, unroll=unroll)

    def scan_fwd(c, s):
        st_s[c, :dk, :] = s.astype(BF16)
        return gf_c * s + kv_s[c, :dk, :]

    def scan_bwd(t, s):
        c = nc - 1 - t
        st_s[c, dk:, :] = s.astype(BF16)
        return gb_c * s + kv_s[c, dk:, :]

    s_f = lax.fori_loop(0, nc, scan_fwd, s0f_ref[0, 0])
    s_b = lax.fori_loop(0, nc, scan_bwd, s0b_ref[0, 0])
    if want_states:
        sfo_ref[0, 0] = s_f
        sbo_ref[0, 0] = s_b

    def chunk_matmuls(c):
        sl = chunk(c)
        q = q_s[sl, :] if rope else q_ref[0, sl, :]
        k = k_s[sl, :] if rope else k_ref[0, sl, :]
        s = lax.dot_general(q, k, (((1,), (1,)), ((), ())), preferred_element_type=F32)
        o = _dot((s * m_s[...]).astype(BF16), v_ref[0, sl, :])
        qd = jnp.concatenate([q * d_s[0], q * d_s[1]], axis=1)
        return o + _dot(qd, st_s[c])

    def chunk_finish(c, o):
        sl = chunk(c)
        o = o * lax.rsqrt(jnp.mean(o * o, axis=-1, keepdims=True) + EPS)
        o_ref[0, sl, :] = (o * _silu(rg_ref[0, sl, :].astype(F32))).astype(o_ref.dtype)

    if want_out:
        pair = 2 if nc % 2 == 0 else 1

        def chunk_outputs(t, prev):
            cur = tuple(chunk_matmuls(t * pair + u) for u in range(pair))
            for u in range(pair):
                chunk_finish((t - 1) * pair + u, prev[u])
            return cur

        first = tuple(chunk_matmuls(u) for u in range(pair))
        last = lax.fori_loop(1, nc // pair, chunk_outputs, first)
        for u in range(pair):
            chunk_finish(nc - pair + u, last[u])


def _retention_call(proj, q_off, decay_logit, s0_f, s0_b, *, rope, want_out, want_states):
    bsz, n, _ = proj.shape
    nh, dk, ch = RET_HEADS, RET_DK, RET_CHUNK
    assert n % ch == 0
    cb = q_off // dk
    col = lambda j: pl.BlockSpec((1, n, dk), functools.partial(lambda b, h, j: (b, 0, cb + j * nh + h), j=j))
    st_spec = pl.BlockSpec((1, 1, dk, dk), lambda b, h: (b, h, 0, 0))
    in_specs = [pl.BlockSpec(memory_space=pltpu.SMEM), col(0), col(1), col(2)]
    args = [decay_logit, proj, proj, proj]
    if want_out:
        in_specs.append(col(3))
        args.append(proj)
    if rope:
        cos, sin = _rope_tables(n)
        in_specs += [_const_spec((n, dk)), _const_spec((n, dk))]
        args += [cos, sin]
    in_specs += [st_spec, st_spec]
    args += [s0_f, s0_b]
    out_specs, out_shape = [], []
    if want_out:
        out_specs.append(pl.BlockSpec((1, n, dk), lambda b, h: (b, 0, h)))
        out_shape.append(jax.ShapeDtypeStruct((bsz, n, RET_WIDTH), BF16))
    if want_states:
        out_specs += [st_spec, st_spec]
        out_shape += [jax.ShapeDtypeStruct((bsz, nh, dk, dk), F32)] * 2
    kern = functools.partial(_retention_kernel, n=n, rope=rope, want_out=want_out, want_states=want_states)
    nc = n // ch
    scratch = [pltpu.VMEM((nc, 2 * dk, dk), F32), pltpu.VMEM((nc, 2 * dk, dk), BF16),
               pltpu.VMEM((ch, ch), F32), pltpu.VMEM((4, ch, dk), BF16)]
    if rope:
        scratch += [pltpu.VMEM((n, dk), BF16), pltpu.VMEM((n, dk), BF16)]
    est = (12 * n * dk * 2 + 2 * n * dk * 4 + nc * 2 * dk * dk * 6 + 16 * ch * ch * 4)
    return pl.pallas_call(
        kern,
        grid=(bsz, nh),
        in_specs=in_specs,
        out_specs=out_specs,
        out_shape=out_shape,
        scratch_shapes=scratch,
        compiler_params=_params(("arbitrary", "arbitrary"), est),
        name="retention",
    )(*args)


def _merge_kernel(uf_ref, up_ref, ur_ref, g0_ref, g1_ref, g2_ref, x_ref, gate_ref,
                  wf_ref, wp_ref, wr_ref, wo_ref, fg_ref, o_ref, *, final):
    m = jax.nn.sigmoid(g0_ref[0].astype(F32)) * _dot(uf_ref[0], wf_ref[...])
    m = m + jax.nn.sigmoid(g1_ref[0].astype(F32)) * _dot(up_ref[0], wp_ref[...])
    m = m + jax.nn.sigmoid(g2_ref[0].astype(F32)) * _dot(ur_ref[0], wr_ref[...])
    y = _dot(m.astype(BF16), wo_ref[...])
    xn = x_ref[0] + gate_ref[0] * y
    if final:
        ms = jnp.mean(xn * xn, axis=-1, keepdims=True)
        xn = xn * lax.rsqrt(ms + EPS) * fg_ref[...]
    o_ref[0] = xn


def _merge_call(u_f, u_p, u_r, proj, x, mod3, mod_row, w_up_f, w_up_p, w_up_r, w_out, layer, final_g, final):
    bsz, n, d = x.shape
    tm = min(MERGE_TM, n)
    row = (lambda b: b) if mod_row is None else (lambda b: mod_row)
    tok = lambda width, cblk: pl.BlockSpec((1, tm, width), lambda b, i: (b, i, cblk))
    gcb = MG_OFF // d
    est = (2 * tm * (F_WIDTH + P_WIDTH + RET_WIDTH) * 2 + 6 * tm * d * 2 + 4 * tm * d * 4
           + (F_WIDTH + P_WIDTH + RET_WIDTH + d) * d * 2 + 4 * tm * d * 4)
    kern = functools.partial(_merge_kernel, final=final)
    return pl.pallas_call(
        kern,
        grid=(bsz, n // tm),
        in_specs=[
            tok(F_WIDTH, 0), tok(P_WIDTH, 0), tok(RET_WIDTH, 0),
            tok(d, gcb), tok(d, gcb + 1), tok(d, gcb + 2),
            tok(d, 0),
            pl.BlockSpec((1, 1, d), lambda b, i: (row(b), 0, 2)),
            _layer_spec(w_up_f.shape, layer), _layer_spec(w_up_p.shape, layer), _layer_spec(w_up_r.shape, layer),
            _layer_spec(w_out.shape, layer),
            _const_spec((1, d)),
        ],
        out_specs=pl.BlockSpec((1, tm, d), lambda b, i: (b, i, 0)),
        out_shape=jax.ShapeDtypeStruct((bsz, n, d), F32),
        compiler_params=_params(("arbitrary", "arbitrary"), est),
        name="merge",
    )(u_f, u_p, u_r, proj, proj, proj, x, mod3, w_up_f, w_up_p, w_up_r, w_out, final_g.reshape(1, d))


def kernel(x, c, ctx, c_ctx, w_ada, b_ada, norm_g, w_in, w_fourier, w_pool, pool_scale, ret_decay_logit,
           w_up_fourier, w_up_pool, w_up_ret, w_out, final_norm_g):
    bsz, n, d = x.shape
    depth = w_ada.shape[0]
    assert d == D_MODEL and bsz + 1 <= MOD_ROWS

    cond_rows = jnp.zeros((MOD_ROWS, d), F32).at[:bsz].set(c).at[bsz].set(c_ctx)
    mod = _ada_call(cond_rows, w_ada, b_ada)

    w_in_b = _wprep_call(w_in)
    w_uf_b, w_upl_b = w_up_fourier.astype(BF16), w_up_pool.astype(BF16)
    w_ur_b, w_out_b = w_up_ret.astype(BF16), w_out.astype(BF16)
    s_zero = jnp.zeros((bsz, RET_HEADS, RET_DK, RET_DK), F32)

    lc = ctx.shape[1]

    def ctx_inproj(l, mod3, col0, ncols):
        p = _inproj_call(ctx.reshape(1, bsz * lc, d), norm_g[l], mod3, bsz, w_in_b, l, col0, ncols)
        return p.reshape(bsz, lc, ncols)

    for l in range(depth):
        last = l == depth - 1
        mod3 = mod[l].reshape(MOD_ROWS, 1, 3 * d)
        if last:
            proj_c = ctx_inproj(l, mod3, R_Q_OFF, MG_OFF - R_Q_OFF)
            s_f, s_b = _retention_call(proj_c, 0, ret_decay_logit[l], s_zero, s_zero,
                                       rope=False, want_out=False, want_states=True)
        else:
            proj_c = ctx_inproj(l, mod3, 0, IN_WIDTH)
            uf_c = _fourier_call(proj_c, w_fourier[l])
            up_c = _pool_call(proj_c, w_pool[l], pool_scale[l])
            ur_c, s_f, s_b = _retention_call(proj_c, R_Q_OFF, ret_decay_logit[l], s_zero, s_zero,
                                             rope=False, want_out=True, want_states=True)
            ctx = _merge_call(uf_c, up_c, ur_c, proj_c, ctx, mod3, bsz, w_uf_b, w_upl_b, w_ur_b,
                              w_out_b, l, final_norm_g, False)
        proj = _inproj_call(x, norm_g[l], mod3, None, w_in_b, l, 0, IN_WIDTH)
        u_f = _fourier_call(proj, w_fourier[l])
        u_p = _pool_call(proj, w_pool[l], pool_scale[l])
        (u_r,) = _retention_call(proj, R_Q_OFF, ret_decay_logit[l], s_f, s_b,
                                 rope=True, want_out=True, want_states=False)
        x = _merge_call(u_f, u_p, u_r, proj, x, mod3, None, w_uf_b, w_upl_b, w_ur_b,
                        w_out_b, l, final_norm_g, last)
    return x
```

```python
import functools
import math

import numpy as np
import jax
import jax.numpy as jnp
from jax import lax
from jax.experimental import pallas as pl
from jax.experimental.pallas import tpu as pltpu

F32 = jnp.float32
BF16 = jnp.bfloat16

D_MODEL = 2048
GRID_W = 64
F_GROUPS = 4
F_WIDTH = D_MODEL // 4
F_GDIM = F_WIDTH // F_GROUPS
POOL_WINDOWS = (2, 4, 8, 16)
P_WIDTH = D_MODEL // 4
P_GDIM = P_WIDTH // len(POOL_WINDOWS)
RET_WIDTH = D_MODEL // 2
RET_HEADS = 8
RET_DK = RET_WIDTH // RET_HEADS
N_BRANCH = 3
ROPE_BASE = 10000.0
EPS = 1e-6

F_X_OFF = 0
F_G_OFF = F_X_OFF + F_WIDTH
P_X_OFF = F_G_OFF + F_WIDTH
P_G_OFF = P_X_OFF + P_WIDTH
R_Q_OFF = P_G_OFF + P_WIDTH
R_K_OFF = R_Q_OFF + RET_WIDTH
R_V_OFF = R_K_OFF + RET_WIDTH
R_G_OFF = R_V_OFF + RET_WIDTH
MG_OFF = R_G_OFF + RET_WIDTH
IN_WIDTH = MG_OFF + N_BRANCH * D_MODEL

V7X_VMEM_BYTES = 64 * 1024 * 1024
VMEM_CAP_BYTES = V7X_VMEM_BYTES - 8 * 1024 * 1024
MOD_ROWS = 8
POOL_HALO = 16
RET_CHUNK = 256
IN_TM, IN_TN = 1024, 1024
IN_TN_WIDE = 1536
NORM_ROWS = 16
MERGE_TM = 512
DFT_TM = 256
DFT_RADIX = 4
POOL_TM = 512


def _vmem_limit(estimate_bytes):
    return int(min(max(estimate_bytes * 5 // 4 + (4 << 20), 32 << 20), VMEM_CAP_BYTES))


def _params(sem, estimate_bytes):
    return pltpu.CompilerParams(dimension_semantics=sem, vmem_limit_bytes=_vmem_limit(estimate_bytes))


def _const_spec(shape):
    nd = len(shape)
    return pl.BlockSpec(shape, lambda *_: (0,) * nd, pipeline_mode=pl.Buffered(1))


def _layer_spec(shape, layer):
    return pl.BlockSpec((None,) + tuple(shape[1:]), lambda *_: (layer, 0, 0), pipeline_mode=pl.Buffered(1))


def _silu(v):
    return v * jax.nn.sigmoid(v)


def _dot(a, b):
    return jnp.dot(a, b, preferred_element_type=F32)


def _ada_kernel(s_ref, w_ref, b_ref, o_ref):
    s = _silu(s_ref[...])
    o_ref[0] = _dot(s.astype(BF16), w_ref[0].astype(BF16)) + b_ref[0]


def _ada_call(cond_rows, w_ada, b_ada):
    depth, d, w3 = w_ada.shape
    tn = 1024
    return pl.pallas_call(
        _ada_kernel,
        grid=(depth, w3 // tn),
        in_specs=[
            pl.BlockSpec((MOD_ROWS, d), lambda l, j: (0, 0)),
            pl.BlockSpec((1, d, tn), lambda l, j: (l, 0, j)),
            pl.BlockSpec((1, 1, tn), lambda l, j: (l, 0, j)),
        ],
        out_specs=pl.BlockSpec((1, MOD_ROWS, tn), lambda l, j: (l, 0, j)),
        out_shape=jax.ShapeDtypeStruct((depth, MOD_ROWS, w3), F32),
        compiler_params=_params(("arbitrary", "arbitrary"), 2 * d * tn * 4 + d * tn * 2),
        name="ada_mod",
    )(cond_rows, w_ada, b_ada.reshape(depth, 1, w3))


def _wprep_kernel(w_ref, o_ref, *, qk_lo, qk_hi):
    j = pl.program_id(1)
    is_qk = jnp.logical_and(j >= qk_lo, j < qk_hi)

    @pl.when(jnp.logical_not(is_qk))
    def _():
        o_ref[0] = w_ref[0].astype(BF16)

    @pl.when(is_qk)
    def _():
        quarter = RET_DK // 4
        for hd in range(w_ref.shape[2] // RET_DK):
            sl = slice(hd * RET_DK, (hd + 1) * RET_DK)
            u = w_ref[0, :, sl]
            lane = lax.broadcasted_iota(jnp.int32, u.shape, 1)
            up = pltpu.roll(u, RET_DK - quarter, 1)
            dn = pltpu.roll(u, quarter, 1)
            v = jnp.where(jnp.logical_and(lane >= quarter, lane < 2 * quarter), up,
                          jnp.where(jnp.logical_and(lane >= 2 * quarter, lane < 3 * quarter), dn, u))
            o_ref[0, :, sl] = v.astype(BF16)


def _wprep_call(w_in):
    depth, d, width = w_in.shape
    tn = IN_TN
    assert R_Q_OFF % tn == 0 and R_V_OFF % tn == 0
    kern = functools.partial(_wprep_kernel, qk_lo=R_Q_OFF // tn, qk_hi=R_V_OFF // tn)
    return pl.pallas_call(
        kern,
        grid=(depth, width // tn),
        in_specs=[pl.BlockSpec((1, d, tn), lambda l, j: (l, 0, j))],
        out_specs=pl.BlockSpec((1, d, tn), lambda l, j: (l, 0, j)),
        out_shape=jax.ShapeDtypeStruct((depth, d, width), BF16),
        compiler_params=_params(("arbitrary", "arbitrary"), 2 * d * tn * 6 + d * tn * 8),
        name="w_prep",
    )(w_in)


def _inproj_kernel(x_ref, g_ref, sc_ref, sh_ref, w_ref, o_ref, h_ref, gain_ref, shift_ref):
    @pl.when(pl.program_id(2) == 0)
    def _():
        rows = NORM_ROWS
        d = x_ref.shape[2]
        gain_ref[...] = jnp.broadcast_to(g_ref[...] * (1.0 + sc_ref[0]), (rows, d))
        shift_ref[...] = jnp.broadcast_to(sh_ref[0], (rows, d))

        def norm_rows(r, carry):
            sl = pl.ds(pl.multiple_of(r * rows, rows), rows)
            x = x_ref[0, sl, :]
            ms = jnp.mean(x * x, axis=-1, keepdims=True)
            h_ref[sl, :] = (x * lax.rsqrt(ms + EPS) * gain_ref[...] + shift_ref[...]).astype(BF16)
            return carry

        lax.fori_loop(0, x_ref.shape[1] // rows, norm_rows, 0, unroll=4)

    o_ref[0] = _dot(h_ref[...], w_ref[...]).astype(o_ref.dtype)


def _inproj_call(x, norm_g, mod3, mod_row, w_bf16, layer, col0, ncols):
    bsz, n, d = x.shape
    tm = min(IN_TM, n)
    tn = IN_TN_WIDE if (col0 % IN_TN_WIDE == 0 and ncols % IN_TN_WIDE == 0) else IN_TN
    cb0 = col0 // tn
    row = (lambda b: b) if mod_row is None else (lambda b: mod_row)
    est = (2 * tm * d * 4 + 2 * d * tn * 2 + 2 * tm * tn * 2 + tm * d * 2 + tm * tn * 4 + 2 * NORM_ROWS * d * 4
           + 8 * NORM_ROWS * d * 4)
    return pl.pallas_call(
        _inproj_kernel,
        grid=(bsz, n // tm, ncols // tn),
        in_specs=[
            pl.BlockSpec((1, tm, d), lambda b, i, j: (b, i, 0)),
            pl.BlockSpec((1, d), lambda b, i, j: (0, 0)),
            pl.BlockSpec((1, 1, d), lambda b, i, j: (row(b), 0, 1)),
            pl.BlockSpec((1, 1, d), lambda b, i, j: (row(b), 0, 0)),
            pl.BlockSpec((None, d, tn), lambda b, i, j: (layer, 0, cb0 + j)),
        ],
        out_specs=pl.BlockSpec((1, tm, tn), lambda b, i, j: (b, i, j)),
        out_shape=jax.ShapeDtypeStruct((bsz, n, ncols), BF16),
        scratch_shapes=[pltpu.VMEM((tm, d), BF16), pltpu.VMEM((NORM_ROWS, d), F32),
                        pltpu.VMEM((NORM_ROWS, d), F32)],
        compiler_params=_params(("arbitrary", "arbitrary", "arbitrary"), est),
        name="in_proj",
    )(x, norm_g.reshape(1, d), mod3, mod3, w_bf16)


@functools.lru_cache(maxsize=None)
def _dft_tables(n, tm, radix):
    m = n // radix
    k = np.arange(m, dtype=np.int64)[None, :]
    r = np.arange(tm, dtype=np.int64)[:, None]
    ang_b = 2.0 * np.pi * ((r * k) % m) / m
    base = (np.arange(m // tm, dtype=np.int64) * tm)[:, None]
    ang_a = 2.0 * np.pi * ((base * k) % m) / m
    rows = np.arange(m, dtype=np.int64)[None, :, None]
    phase = np.arange(radix, dtype=np.int64)[:, None, None]
    ang_t = np.broadcast_to(2.0 * np.pi * ((rows * phase) % n) / n, (radix, m, F_GDIM))
    c = np.arange(F_GDIM, dtype=np.int64)
    ang_c = 2.0 * np.pi * ((c[:, None] * c[None, :]) % F_GDIM) / F_GDIM
    ortho = 1.0 / math.sqrt(n * F_GDIM)
    f = lambda a: np.asarray(a, dtype=np.float32)
    return (f(np.cos(ang_b)), f(np.sin(ang_b)),
            f(np.cos(ang_a))[:, None, :], f(np.sin(ang_a))[:, None, :],
            f(np.cos(ang_t)), f(np.sin(ang_t)),
            f(np.cos(ang_c) * ortho), f(np.sin(ang_c) * ortho))


def _unit_combine(terms):
    acc = None
    for coef, arr in terms:
        if abs(coef) < 1e-9:
            continue
        if abs(abs(coef) - 1.0) < 1e-9:
            term, neg = arr, coef < 0
        else:
            term, neg = arr * coef, False
        if acc is None:
            acc = -term if neg else term
        else:
            acc = acc - term if neg else acc + term
    return acc


def _fourier_kernel(*refs, bsz, tm, radix):
    cb_ref, sb_ref, ca_ref, sa_ref, twc_ref, tws_ref, cc_ref, sc_ref, wf_ref = refs[:9]
    x_refs = refs[9:9 + bsz * radix]
    fg_ref, o_ref, lhs_ref, ab_ref = refs[9 + bsz * radix:]
    gd = F_GDIM

    @pl.when(pl.program_id(0) == 0)
    def _():
        for g in range(F_GROUPS):
            w = wf_ref[g]
            ab_ref[g, :gd, :] = jnp.dot(cc_ref[...], w, preferred_element_type=F32,
                                        precision=lax.Precision.HIGHEST).astype(BF16)
            ab_ref[g, gd:, :] = (-jnp.dot(sc_ref[...], w, preferred_element_type=F32,
                                          precision=lax.Precision.HIGHEST)).astype(BF16)

    ca, sa = ca_ref[0], sa_ref[0]
    cb, sb = cb_ref[...], sb_ref[...]
    lhs_ref[:tm, :] = (ca * cb - sa * sb).astype(BF16)
    lhs_ref[tm:, :] = (sa * cb + ca * sb).astype(BF16)

    for b in range(bsz):
        ec, es = [], []
        for r in range(radix):
            p = _dot(lhs_ref[...], x_refs[b * radix + r][0])
            ec.append(p[:tm])
            es.append(p[tm:])
        for g in range(F_GROUPS):
            sl = slice(g * gd, (g + 1) * gd)
            tc, ts = [ec[0][:, sl]], [es[0][:, sl]]
            for r in range(1, radix):
                c, s = twc_ref[r], tws_ref[r]
                tc.append(c * ec[r][:, sl] - s * es[r][:, sl])
                ts.append(c * es[r][:, sl] + s * ec[r][:, sl])
            for q in range(radix):
                rot = [(math.cos(2.0 * math.pi * q * r / radix), math.sin(2.0 * math.pi * q * r / radix))
                       for r in range(radix)]
                pc = _unit_combine([(a, tc[r]) for r, (a, _) in enumerate(rot)]
                                   + [(-bb, ts[r]) for r, (_, bb) in enumerate(rot)])
                ps = _unit_combine([(a, ts[r]) for r, (a, _) in enumerate(rot)]
                                   + [(bb, tc[r]) for r, (_, bb) in enumerate(rot)])
                cat = jnp.concatenate([pc, ps], axis=1).astype(BF16)
                y = _dot(cat, ab_ref[g])
                o_ref[b, q, :, sl] = (y * _silu(fg_ref[b, q, :, sl].astype(F32))).astype(o_ref.dtype)


def _fourier_call(proj, w_fourier):
    bsz, n, width = proj.shape
    radix = DFT_RADIX
    m = n // radix
    tm = min(DFT_TM, m)
    cb, sb, ca, sa, twc, tws, cc, sc = _dft_tables(n, tm, radix)
    kern = functools.partial(_fourier_kernel, bsz=bsz, tm=tm, radix=radix)
    phases = proj[:, :, F_X_OFF:F_X_OFF + F_WIDTH].reshape(bsz, m, radix, F_WIDTH).transpose(0, 2, 1, 3)
    x_specs = [pl.BlockSpec((1, None, m, F_WIDTH), functools.partial(lambda i, b, r: (b, r, 0, 0), b=b, r=r),
                            pipeline_mode=pl.Buffered(1))
               for b in range(bsz) for r in range(radix)]
    quarters = proj.reshape(bsz, radix, m, width)
    est = (2 * tm * m * 4 + bsz * n * F_WIDTH * 2 + 2 * tm * m * 2 + 4 * bsz * radix * tm * F_WIDTH * 2
           + 4 * radix * tm * F_GDIM * 4 + (2 * radix + 4) * 2 * tm * F_WIDTH * 4 + 3 * tm * m * 4)
    out = pl.pallas_call(
        kern,
        grid=(m // tm,),
        in_specs=[
            _const_spec((tm, m)), _const_spec((tm, m)),
            pl.BlockSpec((1, 1, m), lambda i: (i, 0, 0)),
            pl.BlockSpec((1, 1, m), lambda i: (i, 0, 0)),
            pl.BlockSpec((radix, tm, F_GDIM), lambda i: (0, i, 0)),
            pl.BlockSpec((radix, tm, F_GDIM), lambda i: (0, i, 0)),
            _const_spec((F_GDIM, F_GDIM)), _const_spec((F_GDIM, F_GDIM)),
            _const_spec((F_GROUPS, F_GDIM, F_GDIM)),
            *x_specs,
            pl.BlockSpec((bsz, radix, tm, F_WIDTH), lambda i: (0, 0, i, F_G_OFF // F_WIDTH)),
        ],
        out_specs=pl.BlockSpec((bsz, radix, tm, F_WIDTH), lambda i: (0, 0, i, 0)),
        out_shape=jax.ShapeDtypeStruct((bsz, radix, m, F_WIDTH), BF16),
        scratch_shapes=[pltpu.VMEM((2 * tm, m), BF16),
                        pltpu.VMEM((F_GROUPS, 2 * F_GDIM, F_GDIM), BF16)],
        compiler_params=_params(("arbitrary",), est),
        name="fourier",
    )(cb, sb, ca, sa, twc, tws, cc, sc, w_fourier, *([phases] * (bsz * radix)), quarters)
    return out.reshape(bsz, n, F_WIDTH)


@functools.lru_cache(maxsize=None)
def _pool_bands(tm):
    r = np.arange(tm)[:, None]
    main, prev, nxt = [], [], []
    for w in POOL_WINDOWS:
        hw = w // 2
        band = lambda d: ((d >= -hw) & (d <= hw - 1)).astype(np.float32)
        main.append(band(np.arange(tm)[None, :] - r))
        prev.append(band(np.arange(POOL_HALO)[None, :] - POOL_HALO - r))
        nxt.append(band(np.arange(POOL_HALO)[None, :] + tm - r))
    return np.stack(main), np.stack(prev), np.stack(nxt)


def _pool_kernel(xm_ref, xp_ref, xn_ref, pg_ref, bm_ref, bp_ref, bn_ref, w_ref, ps_ref, o_ref, *, n, tm):
    i = pl.program_id(1)
    has_prev = jnp.where(i > 0, 1.0, 0.0).astype(F32)
    has_next = jnp.where(i < pl.num_programs(1) - 1, 1.0, 0.0).astype(F32)
    t = i * tm + lax.broadcasted_iota(jnp.int32, (tm, P_GDIM), 0)
    for g, w in enumerate(POOL_WINDOWS):
        sl = slice(g * P_GDIM, (g + 1) * P_GDIM)
        xg = xm_ref[0, :, sl]
        s = _dot(bm_ref[g], xg)
        s = s + has_prev * _dot(bp_ref[g], xp_ref[0, :, sl])
        s = s + has_next * _dot(bn_ref[g], xn_ref[0, :, sl])
        cnt = (jnp.minimum(t + w // 2, n) - jnp.maximum(t - w // 2, 0)).astype(F32)
        pooled = s / cnt - xg.astype(F32)
        y = _dot(pooled.astype(BF16), w_ref[g].astype(BF16)) * ps_ref[:, sl]
        o_ref[0, :, sl] = (y * _silu(pg_ref[0, :, sl].astype(F32))).astype(o_ref.dtype)


def _pool_call(proj, w_pool, pool_scale):
    bsz, n, _ = proj.shape
    tm = min(POOL_TM, n)
    bands = [jnp.asarray(a, dtype=BF16) for a in _pool_bands(tm)]
    hb = tm // POOL_HALO
    nhb = n // POOL_HALO
    xcol = P_X_OFF // P_WIDTH
    kern = functools.partial(_pool_kernel, n=n, tm=tm)
    return pl.pallas_call(
        kern,
        grid=(bsz, n // tm),
        in_specs=[
            pl.BlockSpec((1, tm, P_WIDTH), lambda b, i: (b, i, xcol)),
            pl.BlockSpec((1, POOL_HALO, P_WIDTH), lambda b, i: (b, jnp.maximum(i * hb - 1, 0), xcol)),
            pl.BlockSpec((1, POOL_HALO, P_WIDTH), lambda b, i: (b, jnp.minimum((i + 1) * hb, nhb - 1), xcol)),
            pl.BlockSpec((1, tm, P_WIDTH), lambda b, i: (b, i, P_G_OFF // P_WIDTH)),
            _const_spec(bands[0].shape), _const_spec(bands[1].shape), _const_spec(bands[2].shape),
            _const_spec(w_pool.shape),
            _const_spec((1, P_WIDTH)),
        ],
        out_specs=pl.BlockSpec((1, tm, P_WIDTH), lambda b, i: (b, i, 0)),
        out_shape=jax.ShapeDtypeStruct((bsz, n, P_WIDTH), BF16),
        compiler_params=_params(("arbitrary", "arbitrary"), 16 << 20),
        name="pool",
    )(proj, proj, proj, proj, *bands, w_pool, pool_scale.reshape(1, P_WIDTH))


@functools.lru_cache(maxsize=None)
def _rope_tables(n):
    pos = np.arange(n)
    nf = RET_DK // 4
    inv = ROPE_BASE ** (-np.arange(nf, dtype=np.float64) / nf)
    ang_r = (pos // GRID_W)[:, None] * inv[None, :]
    ang_c = (pos % GRID_W)[:, None] * inv[None, :]
    cos = np.concatenate([np.cos(ang_r), np.cos(ang_c)] * 2, axis=1)
    sin = np.concatenate([-np.sin(ang_r), -np.sin(ang_c), np.sin(ang_r), np.sin(ang_c)], axis=1)
    return np.asarray(cos, np.float32), np.asarray(sin, np.float32)


def _log_sigmoid(v):
    return jnp.minimum(v, 0.0) - jnp.log1p(jnp.exp(-jnp.abs(v)))


def _retention_kernel(*refs, n, rope, want_out, want_states):
    it = iter(refs)
    dl_ref = next(it)
    q_ref, k_ref, v_ref = next(it), next(it), next(it)
    rg_ref = next(it) if want_out else None
    cos_ref, sin_ref = (next(it), next(it)) if rope else (None, None)
    s0f_ref, s0b_ref = next(it), next(it)
    o_ref = next(it) if want_out else None
    sfo_ref, sbo_ref = (next(it), next(it)) if want_states else (None, None)
    kv_s, st_s, m_s, d_s = next(it), next(it), next(it), next(it)
    q_s, k_s = (next(it), next(it)) if rope else (None, None)

    ch, dk = RET_CHUNK, RET_DK
    nc = n // ch
    h = pl.program_id(1)
    scale = dk ** -0.5

    def log_gamma(direction, shape):
        return _log_sigmoid(jnp.full(shape, dl_ref[direction, h], F32))

    lg_f, lg_b = log_gamma(0, (ch, dk)), log_gamma(1, (ch, dk))
    row = lax.broadcasted_iota(jnp.int32, (ch, dk), 0).astype(F32)
    d_s[0] = (jnp.exp(lg_f * (row + 1.0)) * scale).astype(BF16)
    d_s[1] = (jnp.exp(lg_b * (ch - row)) * scale).astype(BF16)
    d_s[2] = jnp.exp(lg_f * (ch - 1.0 - row)).astype(BF16)
    d_s[3] = jnp.exp(lg_b * row).astype(BF16)
    gf_c = jnp.exp(log_gamma(0, (dk, dk)) * ch)
    gb_c = jnp.exp(log_gamma(1, (dk, dk)) * ch)
    if want_out:
        diff = (lax.broadcasted_iota(jnp.int32, (ch, ch), 0)
                - lax.broadcasted_iota(jnp.int32, (ch, ch), 1)).astype(F32)
        lf, lb = log_gamma(0, (ch, ch)), log_gamma(1, (ch, ch))
        m_s[...] = scale * (jnp.where(diff >= 0, jnp.exp(lf * jnp.maximum(diff, 0.0)), 0.0)
                            + jnp.where(diff <= 0, jnp.exp(lb * jnp.maximum(-diff, 0.0)), 0.0))

    def chunk(c):
        return pl.ds(c * ch if isinstance(c, int) else pl.multiple_of(c * ch, ch), ch)

    def rotate(u, cos, sin):
        u = u.astype(F32)
        return (u * cos + pltpu.roll(u, dk // 2, 1) * sin).astype(BF16)

    pair = 2 if nc % 2 == 0 else 1

    def decayed_keys(c):
        sl = chunk(c)
        k = k_ref[0, sl, :]
        if rope:
            cos, sin = cos_ref[sl, :], sin_ref[sl, :]
            k = rotate(k, cos, sin)
            k_s[sl, :] = k
            if want_out:
                q_s[sl, :] = rotate(q_ref[0, sl, :], cos, sin)
        return jnp.concatenate([k * d_s[2], k * d_s[3]], axis=1)

    def chunk_state(c, kd):
        kv_s[c] = lax.dot_general(kd, v_ref[0, chunk(c), :], (((0,), (0,)), ((), ())),
                                  preferred_element_type=F32)

    def chunk_states(t, kd_prev):
        for u in range(pair):
            chunk_state((t - 1) * pair + u, kd_prev[u])
        return tuple(decayed_keys(t * pair + u) for u in range(pair))

    kd_last = lax.fori_loop(1, nc // pair, chunk_states, tuple(decayed_keys(u) for u in range(pair)))
    for u in range(pair):
        chunk_state(nc - pair + u, kd_last[u])

    def scan_fwd(c, s):
        st_s[c, :dk, :] = s.astype(BF16)
        return gf_c * s + kv_s[c, :dk, :]

    def scan_bwd(t, s):
        c = nc - 1 - t
        st_s[c, dk:, :] = s.astype(BF16)
        return gb_c * s + kv_s[c, dk:, :]

    s_f = lax.fori_loop(0, nc, scan_fwd, s0f_ref[0, 0])
    s_b = lax.fori_loop(0, nc, scan_bwd, s0b_ref[0, 0])
    if want_states:
        sfo_ref[0, 0] = s_f
        sbo_ref[0, 0] = s_b

    def chunk_matmuls(c):
        sl = chunk(c)
        q = q_s[sl, :] if rope else q_ref[0, sl, :]
        k = k_s[sl, :] if rope else k_ref[0, sl, :]
        s = lax.dot_general(q, k, (((1,), (1,)), ((), ())), preferred_element_type=F32)
        o = _dot((s * m_s[...]).astype(BF16), v_ref[0, sl, :])
        qd = jnp.concatenate([q * d_s[0], q * d_s[1]], axis=1)
        return o + _dot(qd, st_s[c])

    def chunk_finish(c, o):
        sl = chunk(c)
        o = o * lax.rsqrt(jnp.mean(o * o, axis=-1, keepdims=True) + EPS)
        o_ref[0, sl, :] = (o * _silu(rg_ref[0, sl, :].astype(F32))).astype(o_ref.dtype)

    if want_out:
        def chunk_outputs(t, prev):
            cur = tuple(chunk_matmuls(t * pair + u) for u in range(pair))
            for u in range(pair):
                chunk_finish((t - 1) * pair + u, prev[u])
            return cur

        first = tuple(chunk_matmuls(u) for u in range(pair))
        last = lax.fori_loop(1, nc // pair, chunk_outputs, first)
        for u in range(pair):
            chunk_finish(nc - pair + u, last[u])


def _retention_call(proj, q_off, decay_logit, s0_f, s0_b, *, rope, want_out, want_states):
    bsz, n, _ = proj.shape
    nh, dk, ch = RET_HEADS, RET_DK, RET_CHUNK
    assert n % ch == 0
    cb = q_off // dk
    col = lambda j: pl.BlockSpec((1, n, dk), functools.partial(lambda b, h, j: (b, 0, cb + j * nh + h), j=j))
    st_spec = pl.BlockSpec((1, 1, dk, dk), lambda b, h: (b, h, 0, 0))
    in_specs = [pl.BlockSpec(memory_space=pltpu.SMEM), col(0), col(1), col(2)]
    args = [decay_logit, proj, proj, proj]
    if want_out:
        in_specs.append(col(3))
        args.append(proj)
    if rope:
        cos, sin = _rope_tables(n)
        in_specs += [_const_spec((n, dk)), _const_spec((n, dk))]
        args += [cos, sin]
    in_specs += [st_spec, st_spec]
    args += [s0_f, s0_b]
    out_specs, out_shape = [], []
    if want_out:
        out_specs.append(pl.BlockSpec((1, n, dk), lambda b, h: (b, 0, h)))
        out_shape.append(jax.ShapeDtypeStruct((bsz, n, RET_WIDTH), BF16))
    if want_states:
        out_specs += [st_spec, st_spec]
        out_shape += [jax.ShapeDtypeStruct((bsz, nh, dk, dk), F32)] * 2
    kern = functools.partial(_retention_kernel, n=n, rope=rope, want_out=want_out, want_states=want_states)
    nc = n // ch
    scratch = [pltpu.VMEM((nc, 2 * dk, dk), F32), pltpu.VMEM((nc, 2 * dk, dk), BF16),
               pltpu.VMEM((ch, ch), F32), pltpu.VMEM((4, ch, dk), BF16)]
    if rope:
        scratch += [pltpu.VMEM((n, dk), BF16), pltpu.VMEM((n, dk), BF16)]
    est = (12 * n * dk * 2 + 2 * n * dk * 4 + nc * 2 * dk * dk * 6 + 16 * ch * ch * 4)
    return pl.pallas_call(
        kern,
        grid=(bsz, nh),
        in_specs=in_specs,
        out_specs=out_specs,
        out_shape=out_shape,
        scratch_shapes=scratch,
        compiler_params=_params(("arbitrary", "arbitrary"), est),
        name="retention",
    )(*args)


def _merge_kernel(uf_ref, up_ref, ur_ref, g0_ref, g1_ref, g2_ref, x_ref, gate_ref,
                  wf_ref, wp_ref, wr_ref, wo_ref, fg_ref, o_ref, *, final):
    m = jax.nn.sigmoid(g0_ref[0].astype(F32)) * _dot(uf_ref[0], wf_ref[...])
    m = m + jax.nn.sigmoid(g1_ref[0].astype(F32)) * _dot(up_ref[0], wp_ref[...])
    m = m + jax.nn.sigmoid(g2_ref[0].astype(F32)) * _dot(ur_ref[0], wr_ref[...])
    y = _dot(m.astype(BF16), wo_ref[...])
    xn = x_ref[0] + gate_ref[0] * y
    if final:
        ms = jnp.mean(xn * xn, axis=-1, keepdims=True)
        xn = xn * lax.rsqrt(ms + EPS) * fg_ref[...]
    o_ref[0] = xn


def _merge_call(u_f, u_p, u_r, proj, x, mod3, mod_row, w_up_f, w_up_p, w_up_r, w_out, layer, final_g, final):
    bsz, n, d = x.shape
    tm = min(MERGE_TM, n)
    row = (lambda b: b) if mod_row is None else (lambda b: mod_row)
    tok = lambda width, cblk: pl.BlockSpec((1, tm, width), lambda b, i: (b, i, cblk))
    gcb = MG_OFF // d
    est = (2 * tm * (F_WIDTH + P_WIDTH + RET_WIDTH) * 2 + 6 * tm * d * 2 + 4 * tm * d * 4
           + (F_WIDTH + P_WIDTH + RET_WIDTH + d) * d * 2 + 4 * tm * d * 4)
    kern = functools.partial(_merge_kernel, final=final)
    return pl.pallas_call(
        kern,
        grid=(bsz, n // tm),
        in_specs=[
            tok(F_WIDTH, 0), tok(P_WIDTH, 0), tok(RET_WIDTH, 0),
            tok(d, gcb), tok(d, gcb + 1), tok(d, gcb + 2),
            tok(d, 0),
            pl.BlockSpec((1, 1, d), lambda b, i: (row(b), 0, 2)),
            _layer_spec(w_up_f.shape, layer), _layer_spec(w_up_p.shape, layer), _layer_spec(w_up_r.shape, layer),
            _layer_spec(w_out.shape, layer),
            _const_spec((1, d)),
        ],
        out_specs=pl.BlockSpec((1, tm, d), lambda b, i: (b, i, 0)),
        out_shape=jax.ShapeDtypeStruct((bsz, n, d), F32),
        compiler_params=_params(("arbitrary", "arbitrary"), est),
        name="merge",
    )(u_f, u_p, u_r, proj, proj, proj, x, mod3, w_up_f, w_up_p, w_up_r, w_out, final_g.reshape(1, d))


def kernel(x, c, ctx, c_ctx, w_ada, b_ada, norm_g, w_in, w_fourier, w_pool, pool_scale, ret_decay_logit,
           w_up_fourier, w_up_pool, w_up_ret, w_out, final_norm_g):
    bsz, n, d = x.shape
    depth = w_ada.shape[0]
    assert d == D_MODEL and bsz + 1 <= MOD_ROWS

    cond_rows = jnp.zeros((MOD_ROWS, d), F32).at[:bsz].set(c).at[bsz].set(c_ctx)
    mod = _ada_call(cond_rows, w_ada, b_ada)

    w_in_b = _wprep_call(w_in)
    w_uf_b, w_upl_b = w_up_fourier.astype(BF16), w_up_pool.astype(BF16)
    w_ur_b, w_out_b = w_up_ret.astype(BF16), w_out.astype(BF16)
    s_zero = jnp.zeros((bsz, RET_HEADS, RET_DK, RET_DK), F32)

    lc = ctx.shape[1]

    def ctx_inproj(l, mod3, col0, ncols):
        p = _inproj_call(ctx.reshape(1, bsz * lc, d), norm_g[l], mod3, bsz, w_in_b, l, col0, ncols)
        return p.reshape(bsz, lc, ncols)

    for l in range(depth):
        last = l == depth - 1
        mod3 = mod[l].reshape(MOD_ROWS, 1, 3 * d)
        if last:
            proj_c = ctx_inproj(l, mod3, R_Q_OFF, MG_OFF - R_Q_OFF)
            s_f, s_b = _retention_call(proj_c, 0, ret_decay_logit[l], s_zero, s_zero,
                                       rope=False, want_out=False, want_states=True)
        else:
            proj_c = ctx_inproj(l, mod3, 0, IN_WIDTH)
            uf_c = _fourier_call(proj_c, w_fourier[l])
            up_c = _pool_call(proj_c, w_pool[l], pool_scale[l])
            ur_c, s_f, s_b = _retention_call(proj_c, R_Q_OFF, ret_decay_logit[l], s_zero, s_zero,
                                             rope=False, want_out=True, want_states=True)
            ctx = _merge_call(uf_c, up_c, ur_c, proj_c, ctx, mod3, bsz, w_uf_b, w_upl_b, w_ur_b,
                              w_out_b, l, final_norm_g, False)
        proj = _inproj_call(x, norm_g[l], mod3, None, w_in_b, l, 0, IN_WIDTH)
        u_f = _fourier_call(proj, w_fourier[l])
        u_p = _pool_call(proj, w_pool[l], pool_scale[l])
        (u_r,) = _retention_call(proj, R_Q_OFF, ret_decay_logit[l], s_f, s_b,
                                 rope=True, want_out=True, want_states=False)
        x = _merge_call(u_f, u_p, u_r, proj, x, mod3, None, w_uf_b, w_upl_b, w_ur_b,
                        w_out_b, l, final_norm_g, last)
    return x
```

```python
import functools
import math

import numpy as np
import jax
import jax.numpy as jnp
from jax import lax
from jax.experimental import pallas as pl
from jax.experimental.pallas import tpu as pltpu

F32 = jnp.float32
BF16 = jnp.bfloat16

D_MODEL = 2048
GRID_W = 64
F_GROUPS = 4
F_WIDTH = D_MODEL // 4
F_GDIM = F_WIDTH // F_GROUPS
POOL_WINDOWS = (2, 4, 8, 16)
P_WIDTH = D_MODEL // 4
P_GDIM = P_WIDTH // len(POOL_WINDOWS)
RET_WIDTH = D_MODEL // 2
RET_HEADS = 8
RET_DK = RET_WIDTH // RET_HEADS
N_BRANCH = 3
ROPE_BASE = 10000.0
EPS = 1e-6

F_X_OFF = 0
F_G_OFF = F_X_OFF + F_WIDTH
P_X_OFF = F_G_OFF + F_WIDTH
P_G_OFF = P_X_OFF + P_WIDTH
R_Q_OFF = P_G_OFF + P_WIDTH
R_K_OFF = R_Q_OFF + RET_WIDTH
R_V_OFF = R_K_OFF + RET_WIDTH
R_G_OFF = R_V_OFF + RET_WIDTH
MG_OFF = R_G_OFF + RET_WIDTH
IN_WIDTH = MG_OFF + N_BRANCH * D_MODEL

V7X_VMEM_BYTES = 64 * 1024 * 1024
VMEM_CAP_BYTES = V7X_VMEM_BYTES - 8 * 1024 * 1024
MOD_ROWS = 8
POOL_HALO = 16
RET_CHUNK = 256
IN_TM, IN_TN = 1024, 1024
IN_TN_WIDE = 1536
NORM_ROWS = 16
MERGE_TM = 512
DFT_TM = 256
DFT_RADIX = 4
POOL_TM = 512


def _vmem_limit(estimate_bytes):
    return int(min(max(estimate_bytes * 5 // 4 + (4 << 20), 32 << 20), VMEM_CAP_BYTES))


def _params(sem, estimate_bytes):
    return pltpu.CompilerParams(dimension_semantics=sem, vmem_limit_bytes=_vmem_limit(estimate_bytes))


def _const_spec(shape):
    nd = len(shape)
    return pl.BlockSpec(shape, lambda *_: (0,) * nd, pipeline_mode=pl.Buffered(1))


def _layer_spec(shape, layer):
    return pl.BlockSpec((None,) + tuple(shape[1:]), lambda *_: (layer, 0, 0), pipeline_mode=pl.Buffered(1))


def _silu(v):
    return v * jax.nn.sigmoid(v)


def _dot(a, b):
    return jnp.dot(a, b, preferred_element_type=F32)


def _ada_kernel(s_ref, w_ref, b_ref, o_ref):
    s = _silu(s_ref[...])
    o_ref[0] = _dot(s.astype(BF16), w_ref[0].astype(BF16)) + b_ref[0]


def _ada_call(cond_rows, w_ada, b_ada):
    depth, d, w3 = w_ada.shape
    tn = 1024
    return pl.pallas_call(
        _ada_kernel,
        grid=(depth, w3 // tn),
        in_specs=[
            pl.BlockSpec((MOD_ROWS, d), lambda l, j: (0, 0)),
            pl.BlockSpec((1, d, tn), lambda l, j: (l, 0, j)),
            pl.BlockSpec((1, 1, tn), lambda l, j: (l, 0, j)),
        ],
        out_specs=pl.BlockSpec((1, MOD_ROWS, tn), lambda l, j: (l, 0, j)),
        out_shape=jax.ShapeDtypeStruct((depth, MOD_ROWS, w3), F32),
        compiler_params=_params(("arbitrary", "arbitrary"), 2 * d * tn * 4 + d * tn * 2),
        name="ada_mod",
    )(cond_rows, w_ada, b_ada.reshape(depth, 1, w3))


def _wprep_kernel(w_ref, o_ref, *, qk_lo, qk_hi):
    j = pl.program_id(1)
    is_qk = jnp.logical_and(j >= qk_lo, j < qk_hi)

    @pl.when(jnp.logical_not(is_qk))
    def _():
        o_ref[0] = w_ref[0].astype(BF16)

    @pl.when(is_qk)
    def _():
        quarter = RET_DK // 4
        for hd in range(w_ref.shape[2] // RET_DK):
            sl = slice(hd * RET_DK, (hd + 1) * RET_DK)
            u = w_ref[0, :, sl]
            lane = lax.broadcasted_iota(jnp.int32, u.shape, 1)
            up = pltpu.roll(u, RET_DK - quarter, 1)
            dn = pltpu.roll(u, quarter, 1)
            v = jnp.where(jnp.logical_and(lane >= quarter, lane < 2 * quarter), up,
                          jnp.where(jnp.logical_and(lane >= 2 * quarter, lane < 3 * quarter), dn, u))
            o_ref[0, :, sl] = v.astype(BF16)


def _wprep_call(w_in):
    depth, d, width = w_in.shape
    tn = IN_TN
    assert R_Q_OFF % tn == 0 and R_V_OFF % tn == 0
    kern = functools.partial(_wprep_kernel, qk_lo=R_Q_OFF // tn, qk_hi=R_V_OFF // tn)
    return pl.pallas_call(
        kern,
        grid=(depth, width // tn),
        in_specs=[pl.BlockSpec((1, d, tn), lambda l, j: (l, 0, j))],
        out_specs=pl.BlockSpec((1, d, tn), lambda l, j: (l, 0, j)),
        out_shape=jax.ShapeDtypeStruct((depth, d, width), BF16),
        compiler_params=_params(("arbitrary", "arbitrary"), 2 * d * tn * 6 + d * tn * 8),
        name="w_prep",
    )(w_in)


def _inproj_kernel(x0_ref, xn_ref, g_ref, sc0_ref, sh0_ref, scn_ref, shn_ref, w_ref, o_ref,
                   h_ref, gain_ref, shift_ref, *, tiles_per_batch):
    t = pl.program_id(0) * tiles_per_batch + pl.program_id(1)
    j = pl.program_id(2)
    slot = t % 2
    d = x0_ref.shape[2]

    @pl.when(jnp.logical_and(t == 0, j == 0))
    def _():
        rows = NORM_ROWS
        gain_ref[...] = jnp.broadcast_to(g_ref[...] * (1.0 + sc0_ref[0]), (rows, d))
        shift_ref[...] = jnp.broadcast_to(sh0_ref[0], (rows, d))

        def norm_rows(r, carry):
            sl = pl.ds(pl.multiple_of(r * rows, rows), rows)
            x = x0_ref[0, sl, :]
            ms = jnp.mean(x * x, axis=-1, keepdims=True)
            h_ref[0, sl, :] = (x * lax.rsqrt(ms + EPS) * gain_ref[...] + shift_ref[...]).astype(BF16)
            return carry

        lax.fori_loop(0, x0_ref.shape[1] // rows, norm_rows, 0, unroll=4)

    o_ref[0] = _dot(h_ref[slot], w_ref[...]).astype(o_ref.dtype)

    sub = xn_ref.shape[1]
    xs = xn_ref[0]
    ms = jnp.mean(xs * xs, axis=-1, keepdims=True)
    hn = xs * lax.rsqrt(ms + EPS) * (g_ref[...] * (1.0 + scn_ref[0])) + shn_ref[0]
    h_ref[1 - slot, pl.ds(pl.multiple_of(j * sub, sub), sub), :] = hn.astype(BF16)


def _inproj_call(x, norm_g, mod3, mod_row, w_bf16, layer, col0, ncols):
    bsz, n, d = x.shape
    tm = min(IN_TM, n)
    tn = IN_TN_WIDE if (col0 % IN_TN_WIDE == 0 and ncols % IN_TN_WIDE == 0) else IN_TN
    cb0, nj, nt = col0 // tn, ncols // tn, n // tm
    sub = tm // nj
    assert sub * nj == tm and sub % NORM_ROWS == 0
    last_slice = bsz * nt * nj - 1
    row = (lambda b: b) if mod_row is None else (lambda b: mod_row)
    nxt = lambda b, i: jnp.minimum((b * nt + i + 1) // nt, bsz - 1)
    kern = functools.partial(_inproj_kernel, tiles_per_batch=nt)
    est = (tm * d * 4 + 2 * sub * d * 4 + 2 * d * tn * 2 + 2 * tm * tn * 2 + 2 * tm * d * 2 + tm * tn * 4
           + 10 * NORM_ROWS * d * 4 + 4 * sub * d * 4)
    return pl.pallas_call(
        kern,
        grid=(bsz, nt, nj),
        in_specs=[
            pl.BlockSpec((1, tm, d), lambda b, i, j: (0, 0, 0), pipeline_mode=pl.Buffered(1)),
            pl.BlockSpec((1, sub, d), lambda b, i, j: (0, jnp.minimum((b * nt + i + 1) * nj + j, last_slice), 0)),
            pl.BlockSpec((1, d), lambda b, i, j: (0, 0)),
            pl.BlockSpec((1, 1, d), lambda b, i, j: (row(0), 0, 1)),
            pl.BlockSpec((1, 1, d), lambda b, i, j: (row(0), 0, 0)),
            pl.BlockSpec((1, 1, d), lambda b, i, j: (row(nxt(b, i)), 0, 1)),
            pl.BlockSpec((1, 1, d), lambda b, i, j: (row(nxt(b, i)), 0, 0)),
            pl.BlockSpec((None, d, tn), lambda b, i, j: (layer, 0, cb0 + j)),
        ],
        out_specs=pl.BlockSpec((1, tm, tn), lambda b, i, j: (b, i, j)),
        out_shape=jax.ShapeDtypeStruct((bsz, n, ncols), BF16),
        scratch_shapes=[pltpu.VMEM((2, tm, d), BF16), pltpu.VMEM((NORM_ROWS, d), F32),
                        pltpu.VMEM((NORM_ROWS, d), F32)],
        compiler_params=_params(("arbitrary", "arbitrary", "arbitrary"), est),
        name="in_proj",
    )(x, x.reshape(1, bsz * n, d), norm_g.reshape(1, d), mod3, mod3, mod3, mod3, w_bf16)


@functools.lru_cache(maxsize=None)
def _dft_tables(n, tm, radix):
    m = n // radix
    k = np.arange(m, dtype=np.int64)[None, :]
    r = np.arange(tm, dtype=np.int64)[:, None]
    ang_b = 2.0 * np.pi * ((r * k) % m) / m
    base = (np.arange(m // tm, dtype=np.int64) * tm)[:, None]
    ang_a = 2.0 * np.pi * ((base * k) % m) / m
    rows = np.arange(m, dtype=np.int64)[None, :, None]
    phase = np.arange(radix, dtype=np.int64)[:, None, None]
    ang_t = np.broadcast_to(2.0 * np.pi * ((rows * phase) % n) / n, (radix, m, F_GDIM))
    c = np.arange(F_GDIM, dtype=np.int64)
    ang_c = 2.0 * np.pi * ((c[:, None] * c[None, :]) % F_GDIM) / F_GDIM
    ortho = 1.0 / math.sqrt(n * F_GDIM)
    f = lambda a: np.asarray(a, dtype=np.float32)
    return (f(np.cos(ang_b)), f(np.sin(ang_b)),
            f(np.cos(ang_a))[:, None, :], f(np.sin(ang_a))[:, None, :],
            f(np.cos(ang_t)), f(np.sin(ang_t)),
            f(np.cos(ang_c) * ortho), f(np.sin(ang_c) * ortho))


def _unit_combine(terms):
    acc = None
    for coef, arr in terms:
        if abs(coef) < 1e-9:
            continue
        if abs(abs(coef) - 1.0) < 1e-9:
            term, neg = arr, coef < 0
        else:
            term, neg = arr * coef, False
        if acc is None:
            acc = -term if neg else term
        else:
            acc = acc - term if neg else acc + term
    return acc


def _fourier_kernel(*refs, bsz, tm, radix):
    cb_ref, sb_ref, ca_ref, sa_ref, twc_ref, tws_ref, cc_ref, sc_ref, wf_ref = refs[:9]
    x_refs = refs[9:9 + bsz * radix]
    fg_ref, o_ref, lhs_ref, ab_ref = refs[9 + bsz * radix:]
    gd = F_GDIM

    @pl.when(pl.program_id(0) == 0)
    def _():
        for g in range(F_GROUPS):
            w = wf_ref[g]
            ab_ref[g, :gd, :] = jnp.dot(cc_ref[...], w, preferred_element_type=F32,
                                        precision=lax.Precision.HIGHEST).astype(BF16)
            ab_ref[g, gd:, :] = (-jnp.dot(sc_ref[...], w, preferred_element_type=F32,
                                          precision=lax.Precision.HIGHEST)).astype(BF16)

    ca, sa = ca_ref[0], sa_ref[0]
    cb, sb = cb_ref[...], sb_ref[...]
    lhs_ref[:tm, :] = (ca * cb - sa * sb).astype(BF16)
    lhs_ref[tm:, :] = (sa * cb + ca * sb).astype(BF16)

    for b in range(bsz):
        ec, es = [], []
        for r in range(radix):
            p = _dot(lhs_ref[...], x_refs[b * radix + r][0])
            ec.append(p[:tm])
            es.append(p[tm:])
        for g in range(F_GROUPS):
            sl = slice(g * gd, (g + 1) * gd)
            tc, ts = [ec[0][:, sl]], [es[0][:, sl]]
            for r in range(1, radix):
                c, s = twc_ref[r], tws_ref[r]
                tc.append(c * ec[r][:, sl] - s * es[r][:, sl])
                ts.append(c * es[r][:, sl] + s * ec[r][:, sl])
            for q in range(radix):
                rot = [(math.cos(2.0 * math.pi * q * r / radix), math.sin(2.0 * math.pi * q * r / radix))
                       for r in range(radix)]
                pc = _unit_combine([(a, tc[r]) for r, (a, _) in enumerate(rot)]
                                   + [(-bb, ts[r]) for r, (_, bb) in enumerate(rot)])
                ps = _unit_combine([(a, ts[r]) for r, (a, _) in enumerate(rot)]
                                   + [(bb, tc[r]) for r, (_, bb) in enumerate(rot)])
                cat = jnp.concatenate([pc, ps], axis=1).astype(BF16)
                y = _dot(cat, ab_ref[g])
                o_ref[b, q, :, sl] = (y * _silu(fg_ref[b, q, :, sl].astype(F32))).astype(o_ref.dtype)


def _fourier_call(proj, w_fourier):
    bsz, n, width = proj.shape
    radix = DFT_RADIX
    m = n // radix
    tm = min(DFT_TM, m)
    cb, sb, ca, sa, twc, tws, cc, sc = _dft_tables(n, tm, radix)
    kern = functools.partial(_fourier_kernel, bsz=bsz, tm=tm, radix=radix)
    phases = proj[:, :, F_X_OFF:F_X_OFF + F_WIDTH].reshape(bsz, m, radix, F_WIDTH).transpose(0, 2, 1, 3)
    x_specs = [pl.BlockSpec((1, None, m, F_WIDTH), functools.partial(lambda i, b, r: (b, r, 0, 0), b=b, r=r),
                            pipeline_mode=pl.Buffered(1))
               for b in range(bsz) for r in range(radix)]
    quarters = proj.reshape(bsz, radix, m, width)
    est = (2 * tm * m * 4 + bsz * n * F_WIDTH * 2 + 2 * tm * m * 2 + 4 * bsz * radix * tm * F_WIDTH * 2
           + 4 * radix * tm * F_GDIM * 4 + (2 * radix + 4) * 2 * tm * F_WIDTH * 4 + 3 * tm * m * 4)
    out = pl.pallas_call(
        kern,
        grid=(m // tm,),
        in_specs=[
            _const_spec((tm, m)), _const_spec((tm, m)),
            pl.BlockSpec((1, 1, m), lambda i: (i, 0, 0)),
            pl.BlockSpec((1, 1, m), lambda i: (i, 0, 0)),
            pl.BlockSpec((radix, tm, F_GDIM), lambda i: (0, i, 0)),
            pl.BlockSpec((radix, tm, F_GDIM), lambda i: (0, i, 0)),
            _const_spec((F_GDIM, F_GDIM)), _const_spec((F_GDIM, F_GDIM)),
            _const_spec((F_GROUPS, F_GDIM, F_GDIM)),
            *x_specs,
            pl.BlockSpec((bsz, radix, tm, F_WIDTH), lambda i: (0, 0, i, F_G_OFF // F_WIDTH)),
        ],
        out_specs=pl.BlockSpec((bsz, radix, tm, F_WIDTH), lambda i: (0, 0, i, 0)),
        out_shape=jax.ShapeDtypeStruct((bsz, radix, m, F_WIDTH), BF16),
        scratch_shapes=[pltpu.VMEM((2 * tm, m), BF16),
                        pltpu.VMEM((F_GROUPS, 2 * F_GDIM, F_GDIM), BF16)],
        compiler_params=_params(("arbitrary",), est),
        name="fourier",
    )(cb, sb, ca, sa, twc, tws, cc, sc, w_fourier, *([phases] * (bsz * radix)), quarters)
    return out.reshape(bsz, n, F_WIDTH)


@functools.lru_cache(maxsize=None)
def _pool_bands(tm):
    r = np.arange(tm)[:, None]
    main, prev, nxt = [], [], []
    for w in POOL_WINDOWS:
        hw = w // 2
        band = lambda d: ((d >= -hw) & (d <= hw - 1)).astype(np.float32)
        main.append(band(np.arange(tm)[None, :] - r))
        prev.append(band(np.arange(POOL_HALO)[None, :] - POOL_HALO - r))
        nxt.append(band(np.arange(POOL_HALO)[None, :] + tm - r))
    return np.stack(main), np.stack(prev), np.stack(nxt)


def _pool_kernel(xm_ref, xp_ref, xn_ref, pg_ref, bm_ref, bp_ref, bn_ref, w_ref, ps_ref, o_ref, *, n, tm):
    i = pl.program_id(1)
    has_prev = jnp.where(i > 0, 1.0, 0.0).astype(F32)
    has_next = jnp.where(i < pl.num_programs(1) - 1, 1.0, 0.0).astype(F32)
    t = i * tm + lax.broadcasted_iota(jnp.int32, (tm, P_GDIM), 0)
    for g, w in enumerate(POOL_WINDOWS):
        sl = slice(g * P_GDIM, (g + 1) * P_GDIM)
        xg = xm_ref[0, :, sl]
        s = _dot(bm_ref[g], xg)
        s = s + has_prev * _dot(bp_ref[g], xp_ref[0, :, sl])
        s = s + has_next * _dot(bn_ref[g], xn_ref[0, :, sl])
        cnt = (jnp.minimum(t + w // 2, n) - jnp.maximum(t - w // 2, 0)).astype(F32)
        pooled = s / cnt - xg.astype(F32)
        y = _dot(pooled.astype(BF16), w_ref[g].astype(BF16)) * ps_ref[:, sl]
        o_ref[0, :, sl] = (y * _silu(pg_ref[0, :, sl].astype(F32))).astype(o_ref.dtype)


def _pool_call(proj, w_pool, pool_scale):
    bsz, n, _ = proj.shape
    tm = min(POOL_TM, n)
    bands = [jnp.asarray(a, dtype=BF16) for a in _pool_bands(tm)]
    hb = tm // POOL_HALO
    nhb = n // POOL_HALO
    xcol = P_X_OFF // P_WIDTH
    kern = functools.partial(_pool_kernel, n=n, tm=tm)
    return pl.pallas_call(
        kern,
        grid=(bsz, n // tm),
        in_specs=[
            pl.BlockSpec((1, tm, P_WIDTH), lambda b, i: (b, i, xcol)),
            pl.BlockSpec((1, POOL_HALO, P_WIDTH), lambda b, i: (b, jnp.maximum(i * hb - 1, 0), xcol)),
            pl.BlockSpec((1, POOL_HALO, P_WIDTH), lambda b, i: (b, jnp.minimum((i + 1) * hb, nhb - 1), xcol)),
            pl.BlockSpec((1, tm, P_WIDTH), lambda b, i: (b, i, P_G_OFF // P_WIDTH)),
            _const_spec(bands[0].shape), _const_spec(bands[1].shape), _const_spec(bands[2].shape),
            _const_spec(w_pool.shape),
            _const_spec((1, P_WIDTH)),
        ],
        out_specs=pl.BlockSpec((1, tm, P_WIDTH), lambda b, i: (b, i, 0)),
        out_shape=jax.ShapeDtypeStruct((bsz, n, P_WIDTH), BF16),
        compiler_params=_params(("arbitrary", "arbitrary"), 16 << 20),
        name="pool",
    )(proj, proj, proj, proj, *bands, w_pool, pool_scale.reshape(1, P_WIDTH))


@functools.lru_cache(maxsize=None)
def _rope_tables(n):
    pos = np.arange(n)
    nf = RET_DK // 4
    inv = ROPE_BASE ** (-np.arange(nf, dtype=np.float64) / nf)
    ang_r = (pos // GRID_W)[:, None] * inv[None, :]
    ang_c = (pos % GRID_W)[:, None] * inv[None, :]
    cos = np.concatenate([np.cos(ang_r), np.cos(ang_c)] * 2, axis=1)
    sin = np.concatenate([-np.sin(ang_r), -np.sin(ang_c), np.sin(ang_r), np.sin(ang_c)], axis=1)
    return np.asarray(cos, np.float32), np.asarray(sin, np.float32)


def _log_sigmoid(v):
    return jnp.minimum(v, 0.0) - jnp.log1p(jnp.exp(-jnp.abs(v)))


def _retention_kernel(*refs, n, rope, want_out, want_states):
    it = iter(refs)
    dl_ref = next(it)
    q_ref, k_ref, v_ref = next(it), next(it), next(it)
    rg_ref = next(it) if want_out else None
    cos_ref, sin_ref = (next(it), next(it)) if rope else (None, None)
    s0f_ref, s0b_ref = next(it), next(it)
    o_ref = next(it) if want_out else None
    sfo_ref, sbo_ref = (next(it), next(it)) if want_states else (None, None)
    kv_s, st_s, m_s, d_s = next(it), next(it), next(it), next(it)
    q_s, k_s = (next(it), next(it)) if rope else (None, None)

    ch, dk = RET_CHUNK, RET_DK
    nc = n // ch
    h = pl.program_id(1)
    scale = dk ** -0.5

    def log_gamma(direction, shape):
        return _log_sigmoid(jnp.full(shape, dl_ref[direction, h], F32))

    lg_f, lg_b = log_gamma(0, (ch, dk)), log_gamma(1, (ch, dk))
    row = lax.broadcasted_iota(jnp.int32, (ch, dk), 0).astype(F32)
    d_s[0] = (jnp.exp(lg_f * (row + 1.0)) * scale).astype(BF16)
    d_s[1] = (jnp.exp(lg_b * (ch - row)) * scale).astype(BF16)
    d_s[2] = jnp.exp(lg_f * (ch - 1.0 - row)).astype(BF16)
    d_s[3] = jnp.exp(lg_b * row).astype(BF16)
    gf_c = jnp.exp(log_gamma(0, (dk, dk)) * ch)
    gb_c = jnp.exp(log_gamma(1, (dk, dk)) * ch)
    if want_out:
        diff = (lax.broadcasted_iota(jnp.int32, (ch, ch), 0)
                - lax.broadcasted_iota(jnp.int32, (ch, ch), 1)).astype(F32)
        lf, lb = log_gamma(0, (ch, ch)), log_gamma(1, (ch, ch))
        m_s[...] = scale * (jnp.where(diff >= 0, jnp.exp(lf * jnp.maximum(diff, 0.0)), 0.0)
                            + jnp.where(diff <= 0, jnp.exp(lb * jnp.maximum(-diff, 0.0)), 0.0))

    def chunk(c):
        return pl.ds(c * ch if isinstance(c, int) else pl.multiple_of(c * ch, ch), ch)

    def rotate(u, cos, sin):
        u = u.astype(F32)
        return (u * cos + pltpu.roll(u, dk // 2, 1) * sin).astype(BF16)

    pair = 2 if nc % 2 == 0 else 1

    def decayed_keys(c):
        sl = chunk(c)
        k = k_ref[0, sl, :]
        if rope:
            cos, sin = cos_ref[sl, :], sin_ref[sl, :]
            k = rotate(k, cos, sin)
            k_s[sl, :] = k
            if want_out:
                q_s[sl, :] = rotate(q_ref[0, sl, :], cos, sin)
        return jnp.concatenate([k * d_s[2], k * d_s[3]], axis=1)

    def chunk_state(c, kd):
        kv_s[c] = lax.dot_general(kd, v_ref[0, chunk(c), :], (((0,), (0,)), ((), ())),
                                  preferred_element_type=F32)

    def chunk_states(t, kd_prev):
        for u in range(pair):
            chunk_state((t - 1) * pair + u, kd_prev[u])
        return tuple(decayed_keys(t * pair + u) for u in range(pair))

    kd_last = lax.fori_loop(1, nc // pair, chunk_states, tuple(decayed_keys(u) for u in range(pair)))
    for u in range(pair):
        chunk_state(nc - pair + u, kd_last[u])

    def scan_fwd(c, s):
        st_s[c, :dk, :] = s.astype(BF16)
        return gf_c * s + kv_s[c, :dk, :]

    def scan_bwd(t, s):
        c = nc - 1 - t
        st_s[c, dk:, :] = s.astype(BF16)
        return gb_c * s + kv_s[c, dk:, :]

    s_f = lax.fori_loop(0, nc, scan_fwd, s0f_ref[0, 0])
    s_b = lax.fori_loop(0, nc, scan_bwd, s0b_ref[0, 0])
    if want_states:
        sfo_ref[0, 0] = s_f
        sbo_ref[0, 0] = s_b

    def chunk_matmuls(c):
        sl = chunk(c)
        q = q_s[sl, :] if rope else q_ref[0, sl, :]
        k = k_s[sl, :] if rope else k_ref[0, sl, :]
        s = lax.dot_general(q, k, (((1,), (1,)), ((), ())), preferred_element_type=F32)
        o = _dot((s * m_s[...]).astype(BF16), v_ref[0, sl, :])
        qd = jnp.concatenate([q * d_s[0], q * d_s[1]], axis=1)
        return o + _dot(qd, st_s[c])

    def chunk_finish(c, o):
        sl = chunk(c)
        o = o * lax.rsqrt(jnp.mean(o * o, axis=-1, keepdims=True) + EPS)
        o_ref[0, sl, :] = (o * _silu(rg_ref[0, sl, :].astype(F32))).astype(o_ref.dtype)

    if want_out:
        def chunk_outputs(t, prev):
            cur = tuple(chunk_matmuls(t * pair + u) for u in range(pair))
            for u in range(pair):
                chunk_finish((t - 1) * pair + u, prev[u])
            return cur

        first = tuple(chunk_matmuls(u) for u in range(pair))
        last = lax.fori_loop(1, nc // pair, chunk_outputs, first)
        for u in range(pair):
            chunk_finish(nc - pair + u, last[u])


def _retention_call(proj, q_off, decay_logit, s0_f, s0_b, *, rope, want_out, want_states):
    bsz, n, _ = proj.shape
    nh, dk, ch = RET_HEADS, RET_DK, RET_CHUNK
    assert n % ch == 0
    cb = q_off // dk
    col = lambda j: pl.BlockSpec((1, n, dk), functools.partial(lambda b, h, j: (b, 0, cb + j * nh + h), j=j))
    st_spec = pl.BlockSpec((1, 1, dk, dk), lambda b, h: (b, h, 0, 0))
    in_specs = [pl.BlockSpec(memory_space=pltpu.SMEM), col(0), col(1), col(2)]
    args = [decay_logit, proj, proj, proj]
    if want_out:
        in_specs.append(col(3))
        args.append(proj)
    if rope:
        cos, sin = _rope_tables(n)
        in_specs += [_const_spec((n, dk)), _const_spec((n, dk))]
        args += [cos, sin]
    in_specs += [st_spec, st_spec]
    args += [s0_f, s0_b]
    out_specs, out_shape = [], []
    if want_out:
        out_specs.append(pl.BlockSpec((1, n, dk), lambda b, h: (b, 0, h)))
        out_shape.append(jax.ShapeDtypeStruct((bsz, n, RET_WIDTH), BF16))
    if want_states:
        out_specs += [st_spec, st_spec]
        out_shape += [jax.ShapeDtypeStruct((bsz, nh, dk, dk), F32)] * 2
    kern = functools.partial(_retention_kernel, n=n, rope=rope, want_out=want_out, want_states=want_states)
    nc = n // ch
    scratch = [pltpu.VMEM((nc, 2 * dk, dk), F32), pltpu.VMEM((nc, 2 * dk, dk), BF16),
               pltpu.VMEM((ch, ch), F32), pltpu.VMEM((4, ch, dk), BF16)]
    if rope:
        scratch += [pltpu.VMEM((n, dk), BF16), pltpu.VMEM((n, dk), BF16)]
    est = (12 * n * dk * 2 + 2 * n * dk * 4 + nc * 2 * dk * dk * 6 + 16 * ch * ch * 4)
    return pl.pallas_call(
        kern,
        grid=(bsz, nh),
        in_specs=in_specs,
        out_specs=out_specs,
        out_shape=out_shape,
        scratch_shapes=scratch,
        compiler_params=_params(("arbitrary", "arbitrary"), est),
        name="retention",
    )(*args)


def _merge_kernel(uf_ref, up_ref, ur_ref, g0_ref, g1_ref, g2_ref, x_ref, gate_ref,
                  wf_ref, wp_ref, wr_ref, wo_ref, fg_ref, o_ref, *, final):
    m = jax.nn.sigmoid(g0_ref[0].astype(F32)) * _dot(uf_ref[0], wf_ref[...])
    m = m + jax.nn.sigmoid(g1_ref[0].astype(F32)) * _dot(up_ref[0], wp_ref[...])
    m = m + jax.nn.sigmoid(g2_ref[0].astype(F32)) * _dot(ur_ref[0], wr_ref[...])
    y = _dot(m.astype(BF16), wo_ref[...])
    xn = x_ref[0] + gate_ref[0] * y
    if final:
        ms = jnp.mean(xn * xn, axis=-1, keepdims=True)
        xn = xn * lax.rsqrt(ms + EPS) * fg_ref[...]
    o_ref[0] = xn


def _merge_call(u_f, u_p, u_r, proj, x, mod3, mod_row, w_up_f, w_up_p, w_up_r, w_out, layer, final_g, final):
    bsz, n, d = x.shape
    tm = min(MERGE_TM, n)
    row = (lambda b: b) if mod_row is None else (lambda b: mod_row)
    tok = lambda width, cblk: pl.BlockSpec((1, tm, width), lambda b, i: (b, i, cblk))
    gcb = MG_OFF // d
    est = (2 * tm * (F_WIDTH + P_WIDTH + RET_WIDTH) * 2 + 6 * tm * d * 2 + 4 * tm * d * 4
           + (F_WIDTH + P_WIDTH + RET_WIDTH + d) * d * 2 + 4 * tm * d * 4)
    kern = functools.partial(_merge_kernel, final=final)
    return pl.pallas_call(
        kern,
        grid=(bsz, n // tm),
        in_specs=[
            tok(F_WIDTH, 0), tok(P_WIDTH, 0), tok(RET_WIDTH, 0),
            tok(d, gcb), tok(d, gcb + 1), tok(d, gcb + 2),
            tok(d, 0),
            pl.BlockSpec((1, 1, d), lambda b, i: (row(b), 0, 2)),
            _layer_spec(w_up_f.shape, layer), _layer_spec(w_up_p.shape, layer), _layer_spec(w_up_r.shape, layer),
            _layer_spec(w_out.shape, layer),
            _const_spec((1, d)),
        ],
        out_specs=pl.BlockSpec((1, tm, d), lambda b, i: (b, i, 0)),
        out_shape=jax.ShapeDtypeStruct((bsz, n, d), F32),
        compiler_params=_params(("arbitrary", "arbitrary"), est),
        name="merge",
    )(u_f, u_p, u_r, proj, proj, proj, x, mod3, w_up_f, w_up_p, w_up_r, w_out, final_g.reshape(1, d))


def kernel(x, c, ctx, c_ctx, w_ada, b_ada, norm_g, w_in, w_fourier, w_pool, pool_scale, ret_decay_logit,
           w_up_fourier, w_up_pool, w_up_ret, w_out, final_norm_g):
    bsz, n, d = x.shape
    depth = w_ada.shape[0]
    assert d == D_MODEL and bsz + 1 <= MOD_ROWS

    cond_rows = jnp.zeros((MOD_ROWS, d), F32).at[:bsz].set(c).at[bsz].set(c_ctx)
    mod = _ada_call(cond_rows, w_ada, b_ada)

    w_in_b = _wprep_call(w_in)
    w_uf_b, w_upl_b = w_up_fourier.astype(BF16), w_up_pool.astype(BF16)
    w_ur_b, w_out_b = w_up_ret.astype(BF16), w_out.astype(BF16)
    s_zero = jnp.zeros((bsz, RET_HEADS, RET_DK, RET_DK), F32)

    lc = ctx.shape[1]

    def ctx_inproj(l, mod3, col0, ncols):
        p = _inproj_call(ctx.reshape(1, bsz * lc, d), norm_g[l], mod3, bsz, w_in_b, l, col0, ncols)
        return p.reshape(bsz, lc, ncols)

    for l in range(depth):
        last = l == depth - 1
        mod3 = mod[l].reshape(MOD_ROWS, 1, 3 * d)
        if last:
            proj_c = ctx_inproj(l, mod3, R_Q_OFF, MG_OFF - R_Q_OFF)
            s_f, s_b = _retention_call(proj_c, 0, ret_decay_logit[l], s_zero, s_zero,
                                       rope=False, want_out=False, want_states=True)
        else:
            proj_c = ctx_inproj(l, mod3, 0, IN_WIDTH)
            uf_c = _fourier_call(proj_c, w_fourier[l])
            up_c = _pool_call(proj_c, w_pool[l], pool_scale[l])
            ur_c, s_f, s_b = _retention_call(proj_c, R_Q_OFF, ret_decay_logit[l], s_zero, s_zero,
                                             rope=False, want_out=True, want_states=True)
            ctx = _merge_call(uf_c, up_c, ur_c, proj_c, ctx, mod3, bsz, w_uf_b, w_upl_b, w_ur_b,
                              w_out_b, l, final_norm_g, False)
        proj = _inproj_call(x, norm_g[l], mod3, None, w_in_b, l, 0, IN_WIDTH)
        u_f = _fourier_call(proj, w_fourier[l])
        u_p = _pool_call(proj, w_pool[l], pool_scale[l])
        (u_r,) = _retention_call(proj, R_Q_OFF, ret_decay_logit[l], s_f, s_b,
                                 rope=True, want_out=True, want_states=False)
        x = _merge_call(u_f, u_p, u_r, proj, x, mod3, None, w_uf_b, w_upl_b, w_ur_b,
                        w_out_b, l, final_norm_g, last)
    return x
```

```python
import functools
import math

import numpy as np
import jax
import jax.numpy as jnp
from jax import lax
from jax.experimental import pallas as pl
from jax.experimental.pallas import tpu as pltpu

F32 = jnp.float32
BF16 = jnp.bfloat16

D_MODEL = 2048
GRID_W = 64
F_GROUPS = 4
F_WIDTH = D_MODEL // 4
F_GDIM = F_WIDTH // F_GROUPS
POOL_WINDOWS = (2, 4, 8, 16)
P_WIDTH = D_MODEL // 4
P_GDIM = P_WIDTH // len(POOL_WINDOWS)
RET_WIDTH = D_MODEL // 2
RET_HEADS = 8
RET_DK = RET_WIDTH // RET_HEADS
N_BRANCH = 3
ROPE_BASE = 10000.0
EPS = 1e-6

F_X_OFF = 0
F_G_OFF = F_X_OFF + F_WIDTH
P_X_OFF = F_G_OFF + F_WIDTH
P_G_OFF = P_X_OFF + P_WIDTH
R_Q_OFF = P_G_OFF + P_WIDTH
R_K_OFF = R_Q_OFF + RET_WIDTH
R_V_OFF = R_K_OFF + RET_WIDTH
R_G_OFF = R_V_OFF + RET_WIDTH
MG_OFF = R_G_OFF + RET_WIDTH
IN_WIDTH = MG_OFF + N_BRANCH * D_MODEL

V7X_VMEM_BYTES = 64 * 1024 * 1024
VMEM_CAP_BYTES = V7X_VMEM_BYTES - 8 * 1024 * 1024
MOD_ROWS = 8
POOL_HALO = 16
RET_CHUNK = 256
IN_TM, IN_TN = 1024, 1024
IN_TN_WIDE = 1536
NORM_ROWS = 16
MERGE_TM = 512
DFT_TM = 256
DFT_RADIX = 4
POOL_TM = 512


def _vmem_limit(estimate_bytes):
    return int(min(max(estimate_bytes * 5 // 4 + (4 << 20), 32 << 20), VMEM_CAP_BYTES))


def _params(sem, estimate_bytes):
    return pltpu.CompilerParams(dimension_semantics=sem, vmem_limit_bytes=_vmem_limit(estimate_bytes))


def _const_spec(shape):
    nd = len(shape)
    return pl.BlockSpec(shape, lambda *_: (0,) * nd, pipeline_mode=pl.Buffered(1))


def _layer_spec(shape, layer):
    return pl.BlockSpec((None,) + tuple(shape[1:]), lambda *_: (layer, 0, 0), pipeline_mode=pl.Buffered(1))


def _silu(v):
    return v * jax.nn.sigmoid(v)


def _dot(a, b):
    return jnp.dot(a, b, preferred_element_type=F32)


def _ada_kernel(s_ref, w_ref, b_ref, o_ref):
    s = _silu(s_ref[...])
    o_ref[0] = _dot(s.astype(BF16), w_ref[0].astype(BF16)) + b_ref[0]


def _ada_call(cond_rows, w_ada, b_ada):
    depth, d, w3 = w_ada.shape
    tn = 1024
    return pl.pallas_call(
        _ada_kernel,
        grid=(depth, w3 // tn),
        in_specs=[
            pl.BlockSpec((MOD_ROWS, d), lambda l, j: (0, 0)),
            pl.BlockSpec((1, d, tn), lambda l, j: (l, 0, j)),
            pl.BlockSpec((1, 1, tn), lambda l, j: (l, 0, j)),
        ],
        out_specs=pl.BlockSpec((1, MOD_ROWS, tn), lambda l, j: (l, 0, j)),
        out_shape=jax.ShapeDtypeStruct((depth, MOD_ROWS, w3), F32),
        compiler_params=_params(("arbitrary", "arbitrary"), 2 * d * tn * 4 + d * tn * 2),
        name="ada_mod",
    )(cond_rows, w_ada, b_ada.reshape(depth, 1, w3))


def _wprep_kernel(w_ref, o_ref, *, qk_lo, qk_hi):
    j = pl.program_id(1)
    is_qk = jnp.logical_and(j >= qk_lo, j < qk_hi)

    @pl.when(jnp.logical_not(is_qk))
    def _():
        o_ref[0] = w_ref[0].astype(BF16)

    @pl.when(is_qk)
    def _():
        quarter = RET_DK // 4
        for hd in range(w_ref.shape[2] // RET_DK):
            sl = slice(hd * RET_DK, (hd + 1) * RET_DK)
            u = w_ref[0, :, sl]
            lane = lax.broadcasted_iota(jnp.int32, u.shape, 1)
            up = pltpu.roll(u, RET_DK - quarter, 1)
            dn = pltpu.roll(u, quarter, 1)
            v = jnp.where(jnp.logical_and(lane >= quarter, lane < 2 * quarter), up,
                          jnp.where(jnp.logical_and(lane >= 2 * quarter, lane < 3 * quarter), dn, u))
            o_ref[0, :, sl] = v.astype(BF16)


def _wprep_call(w_in):
    depth, d, width = w_in.shape
    tn = IN_TN
    assert R_Q_OFF % tn == 0 and R_V_OFF % tn == 0
    kern = functools.partial(_wprep_kernel, qk_lo=R_Q_OFF // tn, qk_hi=R_V_OFF // tn)
    return pl.pallas_call(
        kern,
        grid=(depth, width // tn),
        in_specs=[pl.BlockSpec((1, d, tn), lambda l, j: (l, 0, j))],
        out_specs=pl.BlockSpec((1, d, tn), lambda l, j: (l, 0, j)),
        out_shape=jax.ShapeDtypeStruct((depth, d, width), BF16),
        compiler_params=_params(("arbitrary", "arbitrary"), 2 * d * tn * 6 + d * tn * 8),
        name="w_prep",
    )(w_in)


def _inproj_kernel(x0_ref, xn_ref, g_ref, sc0_ref, sh0_ref, scn_ref, shn_ref, w_ref, o_ref,
                   h_ref, gain_ref, shift_ref, *, tiles_per_batch):
    t = pl.program_id(0) * tiles_per_batch + pl.program_id(1)
    j = pl.program_id(2)
    slot = t % 2
    d = x0_ref.shape[2]

    @pl.when(jnp.logical_and(t == 0, j == 0))
    def _():
        rows = NORM_ROWS
        gain_ref[...] = jnp.broadcast_to(g_ref[...] * (1.0 + sc0_ref[0]), (rows, d))
        shift_ref[...] = jnp.broadcast_to(sh0_ref[0], (rows, d))

        def norm_rows(r, carry):
            sl = pl.ds(pl.multiple_of(r * rows, rows), rows)
            x = x0_ref[0, sl, :]
            ms = jnp.mean(x * x, axis=-1, keepdims=True)
            h_ref[0, sl, :] = (x * lax.rsqrt(ms + EPS) * gain_ref[...] + shift_ref[...]).astype(BF16)
            return carry

        lax.fori_loop(0, x0_ref.shape[1] // rows, norm_rows, 0, unroll=4)

    o_ref[0] = _dot(h_ref[slot], w_ref[...]).astype(o_ref.dtype)

    sub = xn_ref.shape[1]
    xs = xn_ref[0]
    ms = jnp.mean(xs * xs, axis=-1, keepdims=True)
    hn = xs * lax.rsqrt(ms + EPS) * (g_ref[...] * (1.0 + scn_ref[0])) + shn_ref[0]
    h_ref[1 - slot, pl.ds(pl.multiple_of(j * sub, sub), sub), :] = hn.astype(BF16)


def _inproj_call(x, norm_g, mod3, mod_row, w_bf16, layer, col0, ncols):
    bsz, n, d = x.shape
    tm = min(IN_TM, n)
    tn = IN_TN_WIDE if (col0 % IN_TN_WIDE == 0 and ncols % IN_TN_WIDE == 0) else IN_TN
    cb0, nj, nt = col0 // tn, ncols // tn, n // tm
    sub = tm // nj
    assert sub * nj == tm and sub % NORM_ROWS == 0
    last_slice = bsz * nt * nj - 1
    row = (lambda b: b) if mod_row is None else (lambda b: mod_row)
    nxt = lambda b, i: jnp.minimum((b * nt + i + 1) // nt, bsz - 1)
    kern = functools.partial(_inproj_kernel, tiles_per_batch=nt)
    est = (tm * d * 4 + 2 * sub * d * 4 + 2 * d * tn * 2 + 2 * tm * tn * 2 + 2 * tm * d * 2 + tm * tn * 4
           + 10 * NORM_ROWS * d * 4 + 4 * sub * d * 4)
    return pl.pallas_call(
        kern,
        grid=(bsz, nt, nj),
        in_specs=[
            pl.BlockSpec((1, tm, d), lambda b, i, j: (0, 0, 0), pipeline_mode=pl.Buffered(1)),
            pl.BlockSpec((1, sub, d), lambda b, i, j: (0, jnp.minimum((b * nt + i + 1) * nj + j, last_slice), 0)),
            pl.BlockSpec((1, d), lambda b, i, j: (0, 0)),
            pl.BlockSpec((1, 1, d), lambda b, i, j: (row(0), 0, 1)),
            pl.BlockSpec((1, 1, d), lambda b, i, j: (row(0), 0, 0)),
            pl.BlockSpec((1, 1, d), lambda b, i, j: (row(nxt(b, i)), 0, 1)),
            pl.BlockSpec((1, 1, d), lambda b, i, j: (row(nxt(b, i)), 0, 0)),
            pl.BlockSpec((None, d, tn), lambda b, i, j: (layer, 0, cb0 + j)),
        ],
        out_specs=pl.BlockSpec((1, tm, tn), lambda b, i, j: (b, i, j)),
        out_shape=jax.ShapeDtypeStruct((bsz, n, ncols), BF16),
        scratch_shapes=[pltpu.VMEM((2, tm, d), BF16), pltpu.VMEM((NORM_ROWS, d), F32),
                        pltpu.VMEM((NORM_ROWS, d), F32)],
        compiler_params=_params(("arbitrary", "arbitrary", "arbitrary"), est),
        name="in_proj",
    )(x, x.reshape(1, bsz * n, d), norm_g.reshape(1, d), mod3, mod3, mod3, mod3, w_bf16)


@functools.lru_cache(maxsize=None)
def _dft_tables(n, tm, radix):
    m = n // radix
    k = np.arange(m, dtype=np.int64)[None, :]
    r = np.arange(tm, dtype=np.int64)[:, None]
    ang_b = 2.0 * np.pi * ((r * k) % m) / m
    base = (np.arange(m // tm, dtype=np.int64) * tm)[:, None]
    ang_a = 2.0 * np.pi * ((base * k) % m) / m
    rows = np.arange(m, dtype=np.int64)[None, :, None]
    phase = np.arange(radix, dtype=np.int64)[:, None, None]
    ang_t = np.broadcast_to(2.0 * np.pi * ((rows * phase) % n) / n, (radix, m, F_GDIM))
    c = np.arange(F_GDIM, dtype=np.int64)
    ang_c = 2.0 * np.pi * ((c[:, None] * c[None, :]) % F_GDIM) / F_GDIM
    ortho = 1.0 / math.sqrt(n * F_GDIM)
    f = lambda a: np.asarray(a, dtype=np.float32)
    return (f(np.cos(ang_b)), f(np.sin(ang_b)),
            f(np.cos(ang_a))[:, None, :], f(np.sin(ang_a))[:, None, :],
            f(np.cos(ang_t)), f(np.sin(ang_t)),
            f(np.cos(ang_c) * ortho), f(np.sin(ang_c) * ortho))


def _unit_combine(terms):
    acc = None
    for coef, arr in terms:
        if abs(coef) < 1e-9:
            continue
        if abs(abs(coef) - 1.0) < 1e-9:
            term, neg = arr, coef < 0
        else:
            term, neg = arr * coef, False
        if acc is None:
            acc = -term if neg else term
        else:
            acc = acc - term if neg else acc + term
    return acc


def _fourier_kernel(*refs, bsz, tm, radix):
    cb_ref, sb_ref, ca_ref, sa_ref, twc_ref, tws_ref, cc_ref, sc_ref, wf_ref = refs[:9]
    x_refs = refs[9:9 + bsz * radix]
    fg_ref, o_ref, lhs_ref, ab_ref = refs[9 + bsz * radix:]
    gd = F_GDIM

    @pl.when(pl.program_id(0) == 0)
    def _():
        for g in range(F_GROUPS):
            w = wf_ref[g]
            ab_ref[g, :gd, :] = jnp.dot(cc_ref[...], w, preferred_element_type=F32,
                                        precision=lax.Precision.HIGHEST).astype(BF16)
            ab_ref[g, gd:, :] = (-jnp.dot(sc_ref[...], w, preferred_element_type=F32,
                                          precision=lax.Precision.HIGHEST)).astype(BF16)

    ca, sa = ca_ref[0], sa_ref[0]
    cb, sb = cb_ref[...], sb_ref[...]
    lhs_ref[:tm, :] = (ca * cb - sa * sb).astype(BF16)
    lhs_ref[tm:, :] = (sa * cb + ca * sb).astype(BF16)

    for b in range(bsz):
        ec, es = [], []
        for r in range(radix):
            p = _dot(lhs_ref[...], x_refs[b * radix + r][0])
            ec.append(p[:tm])
            es.append(p[tm:])
        for g in range(F_GROUPS):
            sl = slice(g * gd, (g + 1) * gd)
            tc, ts = [ec[0][:, sl]], [es[0][:, sl]]
            for r in range(1, radix):
                c, s = twc_ref[r], tws_ref[r]
                tc.append(c * ec[r][:, sl] - s * es[r][:, sl])
                ts.append(c * es[r][:, sl] + s * ec[r][:, sl])
            for q in range(radix):
                rot = [(math.cos(2.0 * math.pi * q * r / radix), math.sin(2.0 * math.pi * q * r / radix))
                       for r in range(radix)]
                pc = _unit_combine([(a, tc[r]) for r, (a, _) in enumerate(rot)]
                                   + [(-bb, ts[r]) for r, (_, bb) in enumerate(rot)])
                ps = _unit_combine([(a, ts[r]) for r, (a, _) in enumerate(rot)]
                                   + [(bb, tc[r]) for r, (_, bb) in enumerate(rot)])
                cat = jnp.concatenate([pc, ps], axis=1).astype(BF16)
                y = _dot(cat, ab_ref[g])
                o_ref[b, q, :, sl] = (y * _silu(fg_ref[b, q, :, sl].astype(F32))).astype(o_ref.dtype)


def _fourier_call(proj, w_fourier):
    bsz, n, width = proj.shape
    radix = DFT_RADIX
    m = n // radix
    tm = min(DFT_TM, m)
    cb, sb, ca, sa, twc, tws, cc, sc = _dft_tables(n, tm, radix)
    kern = functools.partial(_fourier_kernel, bsz=bsz, tm=tm, radix=radix)
    phases = proj[:, :, F_X_OFF:F_X_OFF + F_WIDTH].reshape(bsz, m, radix, F_WIDTH).transpose(0, 2, 1, 3)
    x_specs = [pl.BlockSpec((1, None, m, F_WIDTH), functools.partial(lambda i, b, r: (b, r, 0, 0), b=b, r=r),
                            pipeline_mode=pl.Buffered(1))
               for b in range(bsz) for r in range(radix)]
    quarters = proj.reshape(bsz, radix, m, width)
    est = (2 * tm * m * 4 + bsz * n * F_WIDTH * 2 + 2 * tm * m * 2 + 4 * bsz * radix * tm * F_WIDTH * 2
           + 4 * radix * tm * F_GDIM * 4 + (2 * radix + 4) * 2 * tm * F_WIDTH * 4 + 3 * tm * m * 4)
    out = pl.pallas_call(
        kern,
        grid=(m // tm,),
        in_specs=[
            _const_spec((tm, m)), _const_spec((tm, m)),
            pl.BlockSpec((1, 1, m), lambda i: (i, 0, 0)),
            pl.BlockSpec((1, 1, m), lambda i: (i, 0, 0)),
            pl.BlockSpec((radix, tm, F_GDIM), lambda i: (0, i, 0)),
            pl.BlockSpec((radix, tm, F_GDIM), lambda i: (0, i, 0)),
            _const_spec((F_GDIM, F_GDIM)), _const_spec((F_GDIM, F_GDIM)),
            _const_spec((F_GROUPS, F_GDIM, F_GDIM)),
            *x_specs,
            pl.BlockSpec((bsz, radix, tm, F_WIDTH), lambda i: (0, 0, i, F_G_OFF // F_WIDTH)),
        ],
        out_specs=pl.BlockSpec((bsz, radix, tm, F_WIDTH), lambda i: (0, 0, i, 0)),
        out_shape=jax.ShapeDtypeStruct((bsz, radix, m, F_WIDTH), BF16),
        scratch_shapes=[pltpu.VMEM((2 * tm, m), BF16),
                        pltpu.VMEM((F_GROUPS, 2 * F_GDIM, F_GDIM), BF16)],
        compiler_params=_params(("arbitrary",), est),
        name="fourier",
    )(cb, sb, ca, sa, twc, tws, cc, sc, w_fourier, *([phases] * (bsz * radix)), quarters)
    return out.reshape(bsz, n, F_WIDTH)


@functools.lru_cache(maxsize=None)
def _pool_bands(tm):
    r = np.arange(tm)[:, None]
    e = np.arange(POOL_HALO)[:, None]
    main, prev, nxt = [], [], []
    for w in POOL_WINDOWS:
        hw = w // 2
        band = lambda d: ((d >= -hw) & (d <= hw - 1)).astype(np.float32)
        main.append(band(np.arange(tm)[None, :] - r))
        prev.append(band(np.arange(POOL_HALO)[None, :] - POOL_HALO - e))
        nxt.append(band(np.arange(POOL_HALO)[None, :] + POOL_HALO - e))
    return np.stack(main), np.stack(prev), np.stack(nxt)


def _pool_kernel(xm_ref, xp_ref, xn_ref, pg_ref, bm_ref, bp_ref, bn_ref, w_ref, ps_ref, o_ref, *, n, tm):
    i = pl.program_id(1)
    has_prev = jnp.where(i > 0, 1.0, 0.0).astype(F32)
    has_next = jnp.where(i < pl.num_programs(1) - 1, 1.0, 0.0).astype(F32)
    hal = POOL_HALO
    for g, w in enumerate(POOL_WINDOWS):
        sl = slice(g * P_GDIM, (g + 1) * P_GDIM)
        wg = w_ref[g].astype(BF16)

        def emit(rows, s):
            nr = rows.stop - rows.start
            t = i * tm + rows.start + lax.broadcasted_iota(jnp.int32, (nr, P_GDIM), 0)
            cnt = (jnp.minimum(t + w // 2, n) - jnp.maximum(t - w // 2, 0)).astype(F32)
            pooled = s / cnt - xm_ref[0, rows, sl].astype(F32)
            y = _dot(pooled.astype(BF16), wg) * ps_ref[:, sl]
            o_ref[0, rows, sl] = (y * _silu(pg_ref[0, rows, sl].astype(F32))).astype(o_ref.dtype)

        s = _dot(bm_ref[g], xm_ref[0, :, sl])
        emit(slice(0, tm), s)
        emit(slice(0, hal), s[:hal] + has_prev * _dot(bp_ref[g], xp_ref[0, :, sl]))
        emit(slice(tm - hal, tm), s[tm - hal:] + has_next * _dot(bn_ref[g], xn_ref[0, :, sl]))


def _pool_call(proj, w_pool, pool_scale):
    bsz, n, _ = proj.shape
    tm = min(POOL_TM, n)
    bands = [jnp.asarray(a, dtype=BF16) for a in _pool_bands(tm)]
    hb = tm // POOL_HALO
    nhb = n // POOL_HALO
    xcol = P_X_OFF // P_WIDTH
    kern = functools.partial(_pool_kernel, n=n, tm=tm)
    return pl.pallas_call(
        kern,
        grid=(bsz, n // tm),
        in_specs=[
            pl.BlockSpec((1, tm, P_WIDTH), lambda b, i: (b, i, xcol)),
            pl.BlockSpec((1, POOL_HALO, P_WIDTH), lambda b, i: (b, jnp.maximum(i * hb - 1, 0), xcol)),
            pl.BlockSpec((1, POOL_HALO, P_WIDTH), lambda b, i: (b, jnp.minimum((i + 1) * hb, nhb - 1), xcol)),
            pl.BlockSpec((1, tm, P_WIDTH), lambda b, i: (b, i, P_G_OFF // P_WIDTH)),
            _const_spec(bands[0].shape), _const_spec(bands[1].shape), _const_spec(bands[2].shape),
            _const_spec(w_pool.shape),
            _const_spec((1, P_WIDTH)),
        ],
        out_specs=pl.BlockSpec((1, tm, P_WIDTH), lambda b, i: (b, i, 0)),
        out_shape=jax.ShapeDtypeStruct((bsz, n, P_WIDTH), BF16),
        compiler_params=_params(("arbitrary", "arbitrary"), 16 << 20),
        name="pool",
    )(proj, proj, proj, proj, *bands, w_pool, pool_scale.reshape(1, P_WIDTH))


@functools.lru_cache(maxsize=None)
def _rope_tables(n):
    pos = np.arange(n)
    nf = RET_DK // 4
    inv = ROPE_BASE ** (-np.arange(nf, dtype=np.float64) / nf)
    ang_r = (pos // GRID_W)[:, None] * inv[None, :]
    ang_c = (pos % GRID_W)[:, None] * inv[None, :]
    cos = np.concatenate([np.cos(ang_r), np.cos(ang_c)] * 2, axis=1)
    sin = np.concatenate([-np.sin(ang_r), -np.sin(ang_c), np.sin(ang_r), np.sin(ang_c)], axis=1)
    return np.asarray(cos, np.float32), np.asarray(sin, np.float32)


def _log_sigmoid(v):
    return jnp.minimum(v, 0.0) - jnp.log1p(jnp.exp(-jnp.abs(v)))


def _retention_kernel(*refs, n, rope, want_out, want_states):
    it = iter(refs)
    dl_ref = next(it)
    q_ref, k_ref, v_ref = next(it), next(it), next(it)
    rg_ref = next(it) if want_out else None
    cos_ref, sin_ref = (next(it), next(it)) if rope else (None, None)
    s0f_ref, s0b_ref = next(it), next(it)
    o_ref = next(it) if want_out else None
    sfo_ref, sbo_ref = (next(it), next(it)) if want_states else (None, None)
    kv_s, st_s, m_s, d_s = next(it), next(it), next(it), next(it)
    q_s, k_s = (next(it), next(it)) if rope else (None, None)

    ch, dk = RET_CHUNK, RET_DK
    nc = n // ch
    h = pl.program_id(1)
    scale = dk ** -0.5

    def log_gamma(direction, shape):
        return _log_sigmoid(jnp.full(shape, dl_ref[direction, h], F32))

    lg_f, lg_b = log_gamma(0, (ch, dk)), log_gamma(1, (ch, dk))
    row = lax.broadcasted_iota(jnp.int32, (ch, dk), 0).astype(F32)
    d_s[0] = (jnp.exp(lg_f * (row + 1.0)) * scale).astype(BF16)
    d_s[1] = (jnp.exp(lg_b * (ch - row)) * scale).astype(BF16)
    d_s[2] = jnp.exp(lg_f * (ch - 1.0 - row)).astype(BF16)
    d_s[3] = jnp.exp(lg_b * row).astype(BF16)
    gf_c = jnp.exp(log_gamma(0, (dk, dk)) * ch)
    gb_c = jnp.exp(log_gamma(1, (dk, dk)) * ch)
    if want_out:
        diff = (lax.broadcasted_iota(jnp.int32, (ch, ch), 0)
                - lax.broadcasted_iota(jnp.int32, (ch, ch), 1)).astype(F32)
        lf, lb = log_gamma(0, (ch, ch)), log_gamma(1, (ch, ch))
        m_s[...] = scale * (jnp.where(diff >= 0, jnp.exp(lf * jnp.maximum(diff, 0.0)), 0.0)
                            + jnp.where(diff <= 0, jnp.exp(lb * jnp.maximum(-diff, 0.0)), 0.0))

    def chunk(c):
        return pl.ds(c * ch if isinstance(c, int) else pl.multiple_of(c * ch, ch), ch)

    def rotate(u, cos, sin):
        u = u.astype(F32)
        return (u * cos + pltpu.roll(u, dk // 2, 1) * sin).astype(BF16)

    pair = 2 if nc % 2 == 0 else 1

    def decayed_keys(c):
        sl = chunk(c)
        k = k_ref[0, sl, :]
        if rope:
            cos, sin = cos_ref[sl, :], sin_ref[sl, :]
            k = rotate(k, cos, sin)
            k_s[sl, :] = k
            if want_out:
                q_s[sl, :] = rotate(q_ref[0, sl, :], cos, sin)
        return jnp.concatenate([k * d_s[2], k * d_s[3]], axis=1)

    def chunk_state(c, kd):
        kv_s[c] = lax.dot_general(kd, v_ref[0, chunk(c), :], (((0,), (0,)), ((), ())),
                                  preferred_element_type=F32)

    def chunk_states(t, kd_prev):
        for u in range(pair):
            chunk_state((t - 1) * pair + u, kd_prev[u])
        return tuple(decayed_keys(t * pair + u) for u in range(pair))

    kd_last = lax.fori_loop(1, nc // pair, chunk_states, tuple(decayed_keys(u) for u in range(pair)))
    for u in range(pair):
        chunk_state(nc - pair + u, kd_last[u])

    def scan_fwd(c, s):
        st_s[c, :dk, :] = s.astype(BF16)
        return gf_c * s + kv_s[c, :dk, :]

    def scan_bwd(t, s):
        c = nc - 1 - t
        st_s[c, dk:, :] = s.astype(BF16)
        return gb_c * s + kv_s[c, dk:, :]

    s_f = lax.fori_loop(0, nc, scan_fwd, s0f_ref[0, 0])
    s_b = lax.fori_loop(0, nc, scan_bwd, s0b_ref[0, 0])
    if want_states:
        sfo_ref[0, 0] = s_f
        sbo_ref[0, 0] = s_b

    def chunk_matmuls(c):
        sl = chunk(c)
        q = q_s[sl, :] if rope else q_ref[0, sl, :]
        k = k_s[sl, :] if rope else k_ref[0, sl, :]
        s = lax.dot_general(q, k, (((1,), (1,)), ((), ())), preferred_element_type=F32)
        o = _dot((s * m_s[...]).astype(BF16), v_ref[0, sl, :])
        qd = jnp.concatenate([q * d_s[0], q * d_s[1]], axis=1)
        return o + _dot(qd, st_s[c])

    def chunk_finish(c, o):
        sl = chunk(c)
        o = o * lax.rsqrt(jnp.mean(o * o, axis=-1, keepdims=True) + EPS)
        o_ref[0, sl, :] = (o * _silu(rg_ref[0, sl, :].astype(F32))).astype(o_ref.dtype)

    if want_out:
        def chunk_outputs(t, prev):
            cur = tuple(chunk_matmuls(t * pair + u) for u in range(pair))
            for u in range(pair):
                chunk_finish((t - 1) * pair + u, prev[u])
            return cur

        first = tuple(chunk_matmuls(u) for u in range(pair))
        last = lax.fori_loop(1, nc // pair, chunk_outputs, first)
        for u in range(pair):
            chunk_finish(nc - pair + u, last[u])


def _retention_call(proj, q_off, decay_logit, s0_f, s0_b, *, rope, want_out, want_states):
    bsz, n, _ = proj.shape
    nh, dk, ch = RET_HEADS, RET_DK, RET_CHUNK
    assert n % ch == 0
    cb = q_off // dk
    col = lambda j: pl.BlockSpec((1, n, dk), functools.partial(lambda b, h, j: (b, 0, cb + j * nh + h), j=j))
    st_spec = pl.BlockSpec((1, 1, dk, dk), lambda b, h: (b, h, 0, 0))
    in_specs = [pl.BlockSpec(memory_space=pltpu.SMEM), col(0), col(1), col(2)]
    args = [decay_logit, proj, proj, proj]
    if want_out:
        in_specs.append(col(3))
        args.append(proj)
    if rope:
        cos, sin = _rope_tables(n)
        in_specs += [_const_spec((n, dk)), _const_spec((n, dk))]
        args += [cos, sin]
    in_specs += [st_spec, st_spec]
    args += [s0_f, s0_b]
    out_specs, out_shape = [], []
    if want_out:
        out_specs.append(pl.BlockSpec((1, n, dk), lambda b, h: (b, 0, h)))
        out_shape.append(jax.ShapeDtypeStruct((bsz, n, RET_WIDTH), BF16))
    if want_states:
        out_specs += [st_spec, st_spec]
        out_shape += [jax.ShapeDtypeStruct((bsz, nh, dk, dk), F32)] * 2
    kern = functools.partial(_retention_kernel, n=n, rope=rope, want_out=want_out, want_states=want_states)
    nc = n // ch
    scratch = [pltpu.VMEM((nc, 2 * dk, dk), F32), pltpu.VMEM((nc, 2 * dk, dk), BF16),
               pltpu.VMEM((ch, ch), F32), pltpu.VMEM((4, ch, dk), BF16)]
    if rope:
        scratch += [pltpu.VMEM((n, dk), BF16), pltpu.VMEM((n, dk), BF16)]
    est = (12 * n * dk * 2 + 2 * n * dk * 4 + nc * 2 * dk * dk * 6 + 16 * ch * ch * 4)
    return pl.pallas_call(
        kern,
        grid=(bsz, nh),
        in_specs=in_specs,
        out_specs=out_specs,
        out_shape=out_shape,
        scratch_shapes=scratch,
        compiler_params=_params(("arbitrary", "arbitrary"), est),
        name="retention",
    )(*args)


def _merge_kernel(uf_ref, up_ref, ur_ref, g0_ref, g1_ref, g2_ref, x_ref, gate_ref,
                  wf_ref, wp_ref, wr_ref, wo_ref, fg_ref, o_ref, *, final):
    m = jax.nn.sigmoid(g0_ref[0].astype(F32)) * _dot(uf_ref[0], wf_ref[...])
    m = m + jax.nn.sigmoid(g1_ref[0].astype(F32)) * _dot(up_ref[0], wp_ref[...])
    m = m + jax.nn.sigmoid(g2_ref[0].astype(F32)) * _dot(ur_ref[0], wr_ref[...])
    y = _dot(m.astype(BF16), wo_ref[...])
    xn = x_ref[0] + gate_ref[0] * y
    if final:
        ms = jnp.mean(xn * xn, axis=-1, keepdims=True)
        xn = xn * lax.rsqrt(ms + EPS) * fg_ref[...]
    o_ref[0] = xn


def _merge_call(u_f, u_p, u_r, proj, x, mod3, mod_row, w_up_f, w_up_p, w_up_r, w_out, layer, final_g, final):
    bsz, n, d = x.shape
    tm = min(MERGE_TM, n)
    row = (lambda b: b) if mod_row is None else (lambda b: mod_row)
    tok = lambda width, cblk: pl.BlockSpec((1, tm, width), lambda b, i: (b, i, cblk))
    gcb = MG_OFF // d
    est = (2 * tm * (F_WIDTH + P_WIDTH + RET_WIDTH) * 2 + 6 * tm * d * 2 + 4 * tm * d * 4
           + (F_WIDTH + P_WIDTH + RET_WIDTH + d) * d * 2 + 4 * tm * d * 4)
    kern = functools.partial(_merge_kernel, final=final)
    return pl.pallas_call(
        kern,
        grid=(bsz, n // tm),
        in_specs=[
            tok(F_WIDTH, 0), tok(P_WIDTH, 0), tok(RET_WIDTH, 0),
            tok(d, gcb), tok(d, gcb + 1), tok(d, gcb + 2),
            tok(d, 0),
            pl.BlockSpec((1, 1, d), lambda b, i: (row(b), 0, 2)),
            _layer_spec(w_up_f.shape, layer), _layer_spec(w_up_p.shape, layer), _layer_spec(w_up_r.shape, layer),
            _layer_spec(w_out.shape, layer),
            _const_spec((1, d)),
        ],
        out_specs=pl.BlockSpec((1, tm, d), lambda b, i: (b, i, 0)),
        out_shape=jax.ShapeDtypeStruct((bsz, n, d), F32),
        compiler_params=_params(("arbitrary", "arbitrary"), est),
        name="merge",
    )(u_f, u_p, u_r, proj, proj, proj, x, mod3, w_up_f, w_up_p, w_up_r, w_out, final_g.reshape(1, d))


def kernel(x, c, ctx, c_ctx, w_ada, b_ada, norm_g, w_in, w_fourier, w_pool, pool_scale, ret_decay_logit,
           w_up_fourier, w_up_pool, w_up_ret, w_out, final_norm_g):
    bsz, n, d = x.shape
    depth = w_ada.shape[0]
    assert d == D_MODEL and bsz + 1 <= MOD_ROWS

    cond_rows = jnp.zeros((MOD_ROWS, d), F32).at[:bsz].set(c).at[bsz].set(c_ctx)
    mod = _ada_call(cond_rows, w_ada, b_ada)

    w_in_b = _wprep_call(w_in)
    w_uf_b, w_upl_b = w_up_fourier.astype(BF16), w_up_pool.astype(BF16)
    w_ur_b, w_out_b = w_up_ret.astype(BF16), w_out.astype(BF16)
    s_zero = jnp.zeros((bsz, RET_HEADS, RET_DK, RET_DK), F32)

    lc = ctx.shape[1]

    def ctx_inproj(l, mod3, col0, ncols):
        p = _inproj_call(ctx.reshape(1, bsz * lc, d), norm_g[l], mod3, bsz, w_in_b, l, col0, ncols)
        return p.reshape(bsz, lc, ncols)

    for l in range(depth):
        last = l == depth - 1
        mod3 = mod[l].reshape(MOD_ROWS, 1, 3 * d)
        if last:
            proj_c = ctx_inproj(l, mod3, R_Q_OFF, MG_OFF - R_Q_OFF)
            s_f, s_b = _retention_call(proj_c, 0, ret_decay_logit[l], s_zero, s_zero,
                                       rope=False, want_out=False, want_states=True)
        else:
            proj_c = ctx_inproj(l, mod3, 0, IN_WIDTH)
            uf_c = _fourier_call(proj_c, w_fourier[l])
            up_c = _pool_call(proj_c, w_pool[l], pool_scale[l])
            ur_c, s_f, s_b = _retention_call(proj_c, R_Q_OFF, ret_decay_logit[l], s_zero, s_zero,
                                             rope=False, want_out=True, want_states=True)
            ctx = _merge_call(uf_c, up_c, ur_c, proj_c, ctx, mod3, bsz, w_uf_b, w_upl_b, w_ur_b,
                              w_out_b, l, final_norm_g, False)
        proj = _inproj_call(x, norm_g[l], mod3, None, w_in_b, l, 0, IN_WIDTH)
        u_f = _fourier_call(proj, w_fourier[l])
        u_p = _pool_call(proj, w_pool[l], pool_scale[l])
        (u_r,) = _retention_call(proj, R_Q_OFF, ret_decay_logit[l], s_f, s_b,
                                 rope=True, want_out=True, want_states=False)
        x = _merge_call(u_f, u_p, u_r, proj, x, mod3, None, w_uf_b, w_upl_b, w_ur_b,
                        w_out_b, l, final_norm_g, last)
    return x
```

```python
import functools
import math

import numpy as np
import jax
import jax.numpy as jnp
from jax import lax
from jax.experimental import pallas as pl
from jax.experimental.pallas import tpu as pltpu

F32 = jnp.float32
BF16 = jnp.bfloat16

D_MODEL = 2048
GRID_W = 64
F_GROUPS = 4
F_WIDTH = D_MODEL // 4
F_GDIM = F_WIDTH // F_GROUPS
POOL_WINDOWS = (2, 4, 8, 16)
P_WIDTH = D_MODEL // 4
P_GDIM = P_WIDTH // len(POOL_WINDOWS)
RET_WIDTH = D_MODEL // 2
RET_HEADS = 8
RET_DK = RET_WIDTH // RET_HEADS
N_BRANCH = 3
ROPE_BASE = 10000.0
EPS = 1e-6

F_X_OFF = 0
F_G_OFF = F_X_OFF + F_WIDTH
P_X_OFF = F_G_OFF + F_WIDTH
P_G_OFF = P_X_OFF + P_WIDTH
R_Q_OFF = P_G_OFF + P_WIDTH
R_K_OFF = R_Q_OFF + RET_WIDTH
R_V_OFF = R_K_OFF + RET_WIDTH
R_G_OFF = R_V_OFF + RET_WIDTH
MG_OFF = R_G_OFF + RET_WIDTH
IN_WIDTH = MG_OFF + N_BRANCH * D_MODEL

V7X_VMEM_BYTES = 64 * 1024 * 1024
VMEM_CAP_BYTES = V7X_VMEM_BYTES - 8 * 1024 * 1024
MOD_ROWS = 8
POOL_HALO = 16
RET_CHUNK = 256
IN_TM, IN_TN = 1024, 1024
IN_TN_WIDE = 1536
NORM_ROWS = 16
MERGE_TM = 512
DFT_TM = 256
DFT_RADIX = 4
POOL_TM = 512


def _vmem_limit(estimate_bytes):
    return int(min(max(estimate_bytes * 5 // 4 + (4 << 20), 32 << 20), VMEM_CAP_BYTES))


def _params(sem, estimate_bytes):
    return pltpu.CompilerParams(dimension_semantics=sem, vmem_limit_bytes=_vmem_limit(estimate_bytes))


def _const_spec(shape):
    nd = len(shape)
    return pl.BlockSpec(shape, lambda *_: (0,) * nd, pipeline_mode=pl.Buffered(1))


def _layer_spec(shape, layer):
    return pl.BlockSpec((None,) + tuple(shape[1:]), lambda *_: (layer, 0, 0), pipeline_mode=pl.Buffered(1))


def _silu(v):
    return v * jax.nn.sigmoid(v)


def _dot(a, b):
    return jnp.dot(a, b, preferred_element_type=F32)


def _ada_kernel(s_ref, w_ref, b_ref, o_ref):
    s = _silu(s_ref[...])
    o_ref[0] = _dot(s.astype(BF16), w_ref[0].astype(BF16)) + b_ref[0]


def _ada_call(cond_rows, w_ada, b_ada):
    depth, d, w3 = w_ada.shape
    tn = 1024
    return pl.pallas_call(
        _ada_kernel,
        grid=(depth, w3 // tn),
        in_specs=[
            pl.BlockSpec((MOD_ROWS, d), lambda l, j: (0, 0)),
            pl.BlockSpec((1, d, tn), lambda l, j: (l, 0, j)),
            pl.BlockSpec((1, 1, tn), lambda l, j: (l, 0, j)),
        ],
        out_specs=pl.BlockSpec((1, MOD_ROWS, tn), lambda l, j: (l, 0, j)),
        out_shape=jax.ShapeDtypeStruct((depth, MOD_ROWS, w3), F32),
        compiler_params=_params(("arbitrary", "arbitrary"), 2 * d * tn * 4 + d * tn * 2),
        name="ada_mod",
    )(cond_rows, w_ada, b_ada.reshape(depth, 1, w3))


def _wprep_kernel(w_ref, o_ref, *, qk_lo, qk_hi):
    j = pl.program_id(1)
    is_qk = jnp.logical_and(j >= qk_lo, j < qk_hi)

    @pl.when(jnp.logical_not(is_qk))
    def _():
        o_ref[0] = w_ref[0].astype(BF16)

    @pl.when(is_qk)
    def _():
        quarter = RET_DK // 4
        for hd in range(w_ref.shape[2] // RET_DK):
            sl = slice(hd * RET_DK, (hd + 1) * RET_DK)
            u = w_ref[0, :, sl]
            lane = lax.broadcasted_iota(jnp.int32, u.shape, 1)
            up = pltpu.roll(u, RET_DK - quarter, 1)
            dn = pltpu.roll(u, quarter, 1)
            v = jnp.where(jnp.logical_and(lane >= quarter, lane < 2 * quarter), up,
                          jnp.where(jnp.logical_and(lane >= 2 * quarter, lane < 3 * quarter), dn, u))
            o_ref[0, :, sl] = v.astype(BF16)


def _wprep_call(w_in):
    depth, d, width = w_in.shape
    tn = IN_TN
    assert R_Q_OFF % tn == 0 and R_V_OFF % tn == 0
    kern = functools.partial(_wprep_kernel, qk_lo=R_Q_OFF // tn, qk_hi=R_V_OFF // tn)
    return pl.pallas_call(
        kern,
        grid=(depth, width // tn),
        in_specs=[pl.BlockSpec((1, d, tn), lambda l, j: (l, 0, j))],
        out_specs=pl.BlockSpec((1, d, tn), lambda l, j: (l, 0, j)),
        out_shape=jax.ShapeDtypeStruct((depth, d, width), BF16),
        compiler_params=_params(("arbitrary", "arbitrary"), 2 * d * tn * 6 + d * tn * 8),
        name="w_prep",
    )(w_in)


def _inproj_kernel(x0_ref, xn_ref, g_ref, sc0_ref, sh0_ref, scn_ref, shn_ref, w_ref, o_ref,
                   h_ref, gain_ref, shift_ref, *, tiles_per_batch):
    t = pl.program_id(0) * tiles_per_batch + pl.program_id(1)
    j = pl.program_id(2)
    slot = t % 2
    d = x0_ref.shape[2]

    @pl.when(jnp.logical_and(t == 0, j == 0))
    def _():
        rows = NORM_ROWS
        gain_ref[...] = jnp.broadcast_to(g_ref[...] * (1.0 + sc0_ref[0]), (rows, d))
        shift_ref[...] = jnp.broadcast_to(sh0_ref[0], (rows, d))

        def norm_rows(r, carry):
            sl = pl.ds(pl.multiple_of(r * rows, rows), rows)
            x = x0_ref[0, sl, :]
            ms = jnp.mean(x * x, axis=-1, keepdims=True)
            h_ref[0, sl, :] = (x * lax.rsqrt(ms + EPS) * gain_ref[...] + shift_ref[...]).astype(BF16)
            return carry

        lax.fori_loop(0, x0_ref.shape[1] // rows, norm_rows, 0, unroll=4)

    o_ref[0] = _dot(h_ref[slot], w_ref[...]).astype(o_ref.dtype)

    sub = xn_ref.shape[1]
    xs = xn_ref[0]
    ms = jnp.mean(xs * xs, axis=-1, keepdims=True)
    hn = xs * lax.rsqrt(ms + EPS) * (g_ref[...] * (1.0 + scn_ref[0])) + shn_ref[0]
    h_ref[1 - slot, pl.ds(pl.multiple_of(j * sub, sub), sub), :] = hn.astype(BF16)


def _inproj_call(x, norm_g, mod3, mod_row, w_bf16, layer, col0, ncols):
    bsz, n, d = x.shape
    tm = min(IN_TM, n)
    tn = IN_TN_WIDE if (col0 % IN_TN_WIDE == 0 and ncols % IN_TN_WIDE == 0) else IN_TN
    cb0, nj, nt = col0 // tn, ncols // tn, n // tm
    sub = tm // nj
    assert sub * nj == tm and sub % NORM_ROWS == 0
    last_slice = bsz * nt * nj - 1
    row = (lambda b: b) if mod_row is None else (lambda b: mod_row)
    nxt = lambda b, i: jnp.minimum((b * nt + i + 1) // nt, bsz - 1)
    kern = functools.partial(_inproj_kernel, tiles_per_batch=nt)
    est = (tm * d * 4 + 2 * sub * d * 4 + 2 * d * tn * 2 + 2 * tm * tn * 2 + 2 * tm * d * 2 + tm * tn * 4
           + 10 * NORM_ROWS * d * 4 + 4 * sub * d * 4)
    return pl.pallas_call(
        kern,
        grid=(bsz, nt, nj),
        in_specs=[
            pl.BlockSpec((1, tm, d), lambda b, i, j: (0, 0, 0), pipeline_mode=pl.Buffered(1)),
            pl.BlockSpec((1, sub, d), lambda b, i, j: (0, jnp.minimum((b * nt + i + 1) * nj + j, last_slice), 0)),
            pl.BlockSpec((1, d), lambda b, i, j: (0, 0)),
            pl.BlockSpec((1, 1, d), lambda b, i, j: (row(0), 0, 1)),
            pl.BlockSpec((1, 1, d), lambda b, i, j: (row(0), 0, 0)),
            pl.BlockSpec((1, 1, d), lambda b, i, j: (row(nxt(b, i)), 0, 1)),
            pl.BlockSpec((1, 1, d), lambda b, i, j: (row(nxt(b, i)), 0, 0)),
            pl.BlockSpec((None, d, tn), lambda b, i, j: (layer, 0, cb0 + j)),
        ],
        out_specs=pl.BlockSpec((1, tm, tn), lambda b, i, j: (b, i, j)),
        out_shape=jax.ShapeDtypeStruct((bsz, n, ncols), BF16),
        scratch_shapes=[pltpu.VMEM((2, tm, d), BF16), pltpu.VMEM((NORM_ROWS, d), F32),
                        pltpu.VMEM((NORM_ROWS, d), F32)],
        compiler_params=_params(("arbitrary", "arbitrary", "arbitrary"), est),
        name="in_proj",
    )(x, x.reshape(1, bsz * n, d), norm_g.reshape(1, d), mod3, mod3, mod3, mod3, w_bf16)


@functools.lru_cache(maxsize=None)
def _dft_tables(n, tm, radix):
    m = n // radix
    k = np.arange(m, dtype=np.int64)[None, :]
    r = np.arange(tm, dtype=np.int64)[:, None]
    ang_b = 2.0 * np.pi * ((r * k) % m) / m
    base = (np.arange(m // tm, dtype=np.int64) * tm)[:, None]
    ang_a = 2.0 * np.pi * ((base * k) % m) / m
    rows = np.arange(m, dtype=np.int64)[None, :, None]
    phase = np.arange(radix, dtype=np.int64)[:, None, None]
    ang_t = np.broadcast_to(2.0 * np.pi * ((rows * phase) % n) / n, (radix, m, F_GDIM))
    c = np.arange(F_GDIM, dtype=np.int64)
    ang_c = 2.0 * np.pi * ((c[:, None] * c[None, :]) % F_GDIM) / F_GDIM
    ortho = 1.0 / math.sqrt(n * F_GDIM)
    f = lambda a: np.asarray(a, dtype=np.float32)
    return (f(np.cos(ang_b)), f(np.sin(ang_b)),
            f(np.cos(ang_a))[:, None, :], f(np.sin(ang_a))[:, None, :],
            f(np.cos(ang_t)), f(np.sin(ang_t)),
            f(np.cos(ang_c) * ortho), f(np.sin(ang_c) * ortho))


def _unit_combine(terms):
    acc = None
    for coef, arr in terms:
        if abs(coef) < 1e-9:
            continue
        if abs(abs(coef) - 1.0) < 1e-9:
            term, neg = arr, coef < 0
        else:
            term, neg = arr * coef, False
        if acc is None:
            acc = -term if neg else term
        else:
            acc = acc - term if neg else acc + term
    return acc


def _fourier_kernel(*refs, bsz, tm, radix):
    cb_ref, sb_ref, ca_ref, sa_ref, twc_ref, tws_ref, cc_ref, sc_ref, wf_ref = refs[:9]
    x_refs = refs[9:9 + bsz * radix]
    fg_ref, o_ref, lhs_ref, ab_ref = refs[9 + bsz * radix:]
    gd = F_GDIM

    @pl.when(pl.program_id(0) == 0)
    def _():
        for g in range(F_GROUPS):
            w = wf_ref[g]
            ab_ref[g, :gd, :] = jnp.dot(cc_ref[...], w, preferred_element_type=F32,
                                        precision=lax.Precision.HIGHEST).astype(BF16)
            ab_ref[g, gd:, :] = (-jnp.dot(sc_ref[...], w, preferred_element_type=F32,
                                          precision=lax.Precision.HIGHEST)).astype(BF16)

    ca, sa = ca_ref[0], sa_ref[0]
    cb, sb = cb_ref[...], sb_ref[...]
    lhs_ref[:tm, :] = (ca * cb - sa * sb).astype(BF16)
    lhs_ref[tm:, :] = (sa * cb + ca * sb).astype(BF16)

    for b in range(bsz):
        ec, es = [], []
        for r in range(radix):
            p = _dot(lhs_ref[...], x_refs[b * radix + r][0])
            ec.append(p[:tm])
            es.append(p[tm:])
        for g in range(F_GROUPS):
            sl = slice(g * gd, (g + 1) * gd)
            tc, ts = [ec[0][:, sl]], [es[0][:, sl]]
            for r in range(1, radix):
                c, s = twc_ref[r], tws_ref[r]
                tc.append(c * ec[r][:, sl] - s * es[r][:, sl])
                ts.append(c * es[r][:, sl] + s * ec[r][:, sl])
            for q in range(radix):
                rot = [(math.cos(2.0 * math.pi * q * r / radix), math.sin(2.0 * math.pi * q * r / radix))
                       for r in range(radix)]
                pc = _unit_combine([(a, tc[r]) for r, (a, _) in enumerate(rot)]
                                   + [(-bb, ts[r]) for r, (_, bb) in enumerate(rot)])
                ps = _unit_combine([(a, ts[r]) for r, (a, _) in enumerate(rot)]
                                   + [(bb, tc[r]) for r, (_, bb) in enumerate(rot)])
                cat = jnp.concatenate([pc, ps], axis=1).astype(BF16)
                y = _dot(cat, ab_ref[g])
                o_ref[b, q, :, sl] = (y * _silu(fg_ref[b, q, :, sl].astype(F32))).astype(o_ref.dtype)


def _fourier_call(proj, w_fourier):
    bsz, n, width = proj.shape
    radix = DFT_RADIX
    m = n // radix
    tm = min(DFT_TM, m)
    cb, sb, ca, sa, twc, tws, cc, sc = _dft_tables(n, tm, radix)
    kern = functools.partial(_fourier_kernel, bsz=bsz, tm=tm, radix=radix)
    phases = proj[:, :, F_X_OFF:F_X_OFF + F_WIDTH].reshape(bsz, m, radix, F_WIDTH).transpose(0, 2, 1, 3)
    x_specs = [pl.BlockSpec((1, None, m, F_WIDTH), functools.partial(lambda i, b, r: (b, r, 0, 0), b=b, r=r),
                            pipeline_mode=pl.Buffered(1))
               for b in range(bsz) for r in range(radix)]
    quarters = proj.reshape(bsz, radix, m, width)
    est = (2 * tm * m * 4 + bsz * n * F_WIDTH * 2 + 2 * tm * m * 2 + 4 * bsz * radix * tm * F_WIDTH * 2
           + 4 * radix * tm * F_GDIM * 4 + (2 * radix + 4) * 2 * tm * F_WIDTH * 4 + 3 * tm * m * 4)
    out = pl.pallas_call(
        kern,
        grid=(m // tm,),
        in_specs=[
            _const_spec((tm, m)), _const_spec((tm, m)),
            pl.BlockSpec((1, 1, m), lambda i: (i, 0, 0)),
            pl.BlockSpec((1, 1, m), lambda i: (i, 0, 0)),
            pl.BlockSpec((radix, tm, F_GDIM), lambda i: (0, i, 0)),
            pl.BlockSpec((radix, tm, F_GDIM), lambda i: (0, i, 0)),
            _const_spec((F_GDIM, F_GDIM)), _const_spec((F_GDIM, F_GDIM)),
            _const_spec((F_GROUPS, F_GDIM, F_GDIM)),
            *x_specs,
            pl.BlockSpec((bsz, radix, tm, F_WIDTH), lambda i: (0, 0, i, F_G_OFF // F_WIDTH)),
        ],
        out_specs=pl.BlockSpec((bsz, radix, tm, F_WIDTH), lambda i: (0, 0, i, 0)),
        out_shape=jax.ShapeDtypeStruct((bsz, radix, m, F_WIDTH), BF16),
        scratch_shapes=[pltpu.VMEM((2 * tm, m), BF16),
                        pltpu.VMEM((F_GROUPS, 2 * F_GDIM, F_GDIM), BF16)],
        compiler_params=_params(("arbitrary",), est),
        name="fourier",
    )(cb, sb, ca, sa, twc, tws, cc, sc, w_fourier, *([phases] * (bsz * radix)), quarters)
    return out.reshape(bsz, n, F_WIDTH)


@functools.lru_cache(maxsize=None)
def _pool_bands(tm):
    r = np.arange(tm)[:, None]
    main, prev, nxt = [], [], []
    for w in POOL_WINDOWS:
        hw = w // 2
        band = lambda d: ((d >= -hw) & (d <= hw - 1)).astype(np.float32)
        main.append(band(np.arange(tm)[None, :] - r))
        prev.append(band(np.arange(POOL_HALO)[None, :] - POOL_HALO - r))
        nxt.append(band(np.arange(POOL_HALO)[None, :] + tm - r))
    return np.stack(main), np.stack(prev), np.stack(nxt)


def _pool_kernel(xm_ref, xp_ref, xn_ref, pg_ref, bm_ref, bp_ref, bn_ref, w_ref, ps_ref, o_ref, *, n, tm):
    i = pl.program_id(1)
    has_prev = jnp.where(i > 0, 1.0, 0.0).astype(F32)
    has_next = jnp.where(i < pl.num_programs(1) - 1, 1.0, 0.0).astype(F32)
    t = i * tm + lax.broadcasted_iota(jnp.int32, (tm, P_GDIM), 0)
    for g, w in enumerate(POOL_WINDOWS):
        sl = slice(g * P_GDIM, (g + 1) * P_GDIM)
        xg = xm_ref[0, :, sl]
        s = _dot(bm_ref[g], xg)
        s = s + has_prev * _dot(bp_ref[g], xp_ref[0, :, sl])
        s = s + has_next * _dot(bn_ref[g], xn_ref[0, :, sl])
        cnt = (jnp.minimum(t + w // 2, n) - jnp.maximum(t - w // 2, 0)).astype(F32)
        pooled = s / cnt - xg.astype(F32)
        y = _dot(pooled.astype(BF16), w_ref[g].astype(BF16)) * ps_ref[:, sl]
        o_ref[0, :, sl] = (y * _silu(pg_ref[0, :, sl].astype(F32))).astype(o_ref.dtype)


def _pool_call(proj, w_pool, pool_scale):
    bsz, n, _ = proj.shape
    tm = min(POOL_TM, n)
    bands = [jnp.asarray(a, dtype=BF16) for a in _pool_bands(tm)]
    hb = tm // POOL_HALO
    nhb = n // POOL_HALO
    xcol = P_X_OFF // P_WIDTH
    kern = functools.partial(_pool_kernel, n=n, tm=tm)
    return pl.pallas_call(
        kern,
        grid=(bsz, n // tm),
        in_specs=[
            pl.BlockSpec((1, tm, P_WIDTH), lambda b, i: (b, i, xcol)),
            pl.BlockSpec((1, POOL_HALO, P_WIDTH), lambda b, i: (b, jnp.maximum(i * hb - 1, 0), xcol)),
            pl.BlockSpec((1, POOL_HALO, P_WIDTH), lambda b, i: (b, jnp.minimum((i + 1) * hb, nhb - 1), xcol)),
            pl.BlockSpec((1, tm, P_WIDTH), lambda b, i: (b, i, P_G_OFF // P_WIDTH)),
            _const_spec(bands[0].shape), _const_spec(bands[1].shape), _const_spec(bands[2].shape),
            _const_spec(w_pool.shape),
            _const_spec((1, P_WIDTH)),
        ],
        out_specs=pl.BlockSpec((1, tm, P_WIDTH), lambda b, i: (b, i, 0)),
        out_shape=jax.ShapeDtypeStruct((bsz, n, P_WIDTH), BF16),
        compiler_params=_params(("arbitrary", "arbitrary"), 16 << 20),
        name="pool",
    )(proj, proj, proj, proj, *bands, w_pool, pool_scale.reshape(1, P_WIDTH))


@functools.lru_cache(maxsize=None)
def _rope_tables(n):
    pos = np.arange(n)
    nf = RET_DK // 4
    inv = ROPE_BASE ** (-np.arange(nf, dtype=np.float64) / nf)
    ang_r = (pos // GRID_W)[:, None] * inv[None, :]
    ang_c = (pos % GRID_W)[:, None] * inv[None, :]
    cos = np.concatenate([np.cos(ang_r), np.cos(ang_c)] * 2, axis=1)
    sin = np.concatenate([-np.sin(ang_r), -np.sin(ang_c), np.sin(ang_r), np.sin(ang_c)], axis=1)
    return np.asarray(cos, np.float32), np.asarray(sin, np.float32)


def _log_sigmoid(v):
    return jnp.minimum(v, 0.0) - jnp.log1p(jnp.exp(-jnp.abs(v)))


def _retention_kernel(*refs, n, rope, want_out, want_states):
    it = iter(refs)
    dl_ref = next(it)
    q_ref, k_ref, v_ref = next(it), next(it), next(it)
    rg_ref = next(it) if want_out else None
    cos_ref, sin_ref = (next(it), next(it)) if rope else (None, None)
    s0f_ref, s0b_ref = next(it), next(it)
    o_ref = next(it) if want_out else None
    sfo_ref, sbo_ref = (next(it), next(it)) if want_states else (None, None)
    kv_s, st_s, m_s, d_s = next(it), next(it), next(it), next(it)
    q_s, k_s = (next(it), next(it)) if rope else (None, None)

    ch, dk = RET_CHUNK, RET_DK
    nc = n // ch
    h = pl.program_id(1)
    scale = dk ** -0.5

    def log_gamma(direction, shape):
        return _log_sigmoid(jnp.full(shape, dl_ref[direction, h], F32))

    lg_f, lg_b = log_gamma(0, (ch, dk)), log_gamma(1, (ch, dk))
    row = lax.broadcasted_iota(jnp.int32, (ch, dk), 0).astype(F32)
    d_s[0] = (jnp.exp(lg_f * (row + 1.0)) * scale).astype(BF16)
    d_s[1] = (jnp.exp(lg_b * (ch - row)) * scale).astype(BF16)
    d_s[2] = jnp.exp(lg_f * (ch - 1.0 - row)).astype(BF16)
    d_s[3] = jnp.exp(lg_b * row).astype(BF16)
    gf_c = jnp.exp(log_gamma(0, (dk, dk)) * ch)
    gb_c = jnp.exp(log_gamma(1, (dk, dk)) * ch)
    if want_out:
        diff = (lax.broadcasted_iota(jnp.int32, (ch, ch), 0)
                - lax.broadcasted_iota(jnp.int32, (ch, ch), 1)).astype(F32)
        lf, lb = log_gamma(0, (ch, ch)), log_gamma(1, (ch, ch))
        m_s[...] = scale * (jnp.where(diff >= 0, jnp.exp(lf * jnp.maximum(diff, 0.0)), 0.0)
                            + jnp.where(diff <= 0, jnp.exp(lb * jnp.maximum(-diff, 0.0)), 0.0))

    def chunk(c):
        return pl.ds(c * ch if isinstance(c, int) else pl.multiple_of(c * ch, ch), ch)

    def rotate(u, cos, sin):
        u = u.astype(F32)
        return (u * cos + pltpu.roll(u, dk // 2, 1) * sin).astype(BF16)

    pair = 2 if nc % 2 == 0 else 1

    def decayed_keys(c):
        sl = chunk(c)
        k = k_ref[0, sl, :]
        if rope:
            cos, sin = cos_ref[sl, :], sin_ref[sl, :]
            k = rotate(k, cos, sin)
            k_s[sl, :] = k
            if want_out:
                q_s[sl, :] = rotate(q_ref[0, sl, :], cos, sin)
        return jnp.concatenate([k * d_s[2], k * d_s[3]], axis=1)

    def chunk_state(c, kd):
        kv_s[c] = lax.dot_general(kd, v_ref[0, chunk(c), :], (((0,), (0,)), ((), ())),
                                  preferred_element_type=F32)

    def chunk_states(t, kd_prev):
        for u in range(pair):
            chunk_state((t - 1) * pair + u, kd_prev[u])
        return tuple(decayed_keys(t * pair + u) for u in range(pair))

    kd_last = lax.fori_loop(1, nc // pair, chunk_states, tuple(decayed_keys(u) for u in range(pair)))
    for u in range(pair):
        chunk_state(nc - pair + u, kd_last[u])

    def scan_fwd(c, s):
        st_s[c, :dk, :] = s.astype(BF16)
        return gf_c * s + kv_s[c, :dk, :]

    def scan_bwd(t, s):
        c = nc - 1 - t
        st_s[c, dk:, :] = s.astype(BF16)
        return gb_c * s + kv_s[c, dk:, :]

    s_f = lax.fori_loop(0, nc, scan_fwd, s0f_ref[0, 0])
    s_b = lax.fori_loop(0, nc, scan_bwd, s0b_ref[0, 0])
    if want_states:
        sfo_ref[0, 0] = s_f
        sbo_ref[0, 0] = s_b

    def chunk_matmuls(c):
        sl = chunk(c)
        q = q_s[sl, :] if rope else q_ref[0, sl, :]
        k = k_s[sl, :] if rope else k_ref[0, sl, :]
        s = lax.dot_general(q, k, (((1,), (1,)), ((), ())), preferred_element_type=F32)
        o = _dot((s * m_s[...]).astype(BF16), v_ref[0, sl, :])
        qd = jnp.concatenate([q * d_s[0], q * d_s[1]], axis=1)
        return o + _dot(qd, st_s[c])

    def chunk_finish(c, o):
        sl = chunk(c)
        o = o * lax.rsqrt(jnp.mean(o * o, axis=-1, keepdims=True) + EPS)
        o_ref[0, sl, :] = (o * _silu(rg_ref[0, sl, :].astype(F32))).astype(o_ref.dtype)

    if want_out:
        def chunk_outputs(t, prev):
            cur = tuple(chunk_matmuls(t * pair + u) for u in range(pair))
            for u in range(pair):
                chunk_finish((t - 1) * pair + u, prev[u])
            return cur

        first = tuple(chunk_matmuls(u) for u in range(pair))
        last = lax.fori_loop(1, nc // pair, chunk_outputs, first)
        for u in range(pair):
            chunk_finish(nc - pair + u, last[u])


def _retention_call(proj, q_off, decay_logit, s0_f, s0_b, *, rope, want_out, want_states):
    bsz, n, _ = proj.shape
    nh, dk, ch = RET_HEADS, RET_DK, RET_CHUNK
    assert n % ch == 0
    cb = q_off // dk
    col = lambda j: pl.BlockSpec((1, n, dk), functools.partial(lambda b, h, j: (b, 0, cb + j * nh + h), j=j))
    st_spec = pl.BlockSpec((1, 1, dk, dk), lambda b, h: (b, h, 0, 0))
    in_specs = [pl.BlockSpec(memory_space=pltpu.SMEM), col(0), col(1), col(2)]
    args = [decay_logit, proj, proj, proj]
    if want_out:
        in_specs.append(col(3))
        args.append(proj)
    if rope:
        cos, sin = _rope_tables(n)
        in_specs += [_const_spec((n, dk)), _const_spec((n, dk))]
        args += [cos, sin]
    in_specs += [st_spec, st_spec]
    args += [s0_f, s0_b]
    out_specs, out_shape = [], []
    if want_out:
        out_specs.append(pl.BlockSpec((1, n, dk), lambda b, h: (b, 0, h)))
        out_shape.append(jax.ShapeDtypeStruct((bsz, n, RET_WIDTH), BF16))
    if want_states:
        out_specs += [st_spec, st_spec]
        out_shape += [jax.ShapeDtypeStruct((bsz, nh, dk, dk), F32)] * 2
    kern = functools.partial(_retention_kernel, n=n, rope=rope, want_out=want_out, want_states=want_states)
    nc = n // ch
    scratch = [pltpu.VMEM((nc, 2 * dk, dk), F32), pltpu.VMEM((nc, 2 * dk, dk), BF16),
               pltpu.VMEM((ch, ch), F32), pltpu.VMEM((4, ch, dk), BF16)]
    if rope:
        scratch += [pltpu.VMEM((n, dk), BF16), pltpu.VMEM((n, dk), BF16)]
    est = (12 * n * dk * 2 + 2 * n * dk * 4 + nc * 2 * dk * dk * 6 + 16 * ch * ch * 4)
    return pl.pallas_call(
        kern,
        grid=(bsz, nh),
        in_specs=in_specs,
        out_specs=out_specs,
        out_shape=out_shape,
        scratch_shapes=scratch,
        compiler_params=_params(("arbitrary", "arbitrary"), est),
        name="retention",
    )(*args)


def _merge_kernel(uf_ref, up_ref, ur_ref, g0_ref, g1_ref, g2_ref, x_ref, gate_ref,
                  wf_ref, wp_ref, wr_ref, wo_ref, fg_ref, o_ref, *, final):
    m = jax.nn.sigmoid(g0_ref[0].astype(F32)) * _dot(uf_ref[0], wf_ref[...])
    m = m + jax.nn.sigmoid(g1_ref[0].astype(F32)) * _dot(up_ref[0], wp_ref[...])
    m = m + jax.nn.sigmoid(g2_ref[0].astype(F32)) * _dot(ur_ref[0], wr_ref[...])
    y = _dot(m.astype(BF16), wo_ref[...])
    xn = x_ref[0] + gate_ref[0] * y
    if final:
        ms = jnp.mean(xn * xn, axis=-1, keepdims=True)
        xn = xn * lax.rsqrt(ms + EPS) * fg_ref[...]
    o_ref[0] = xn


def _merge_call(u_f, u_p, u_r, proj, x, mod3, mod_row, w_up_f, w_up_p, w_up_r, w_out, layer, final_g, final):
    bsz, n, d = x.shape
    tm = min(MERGE_TM, n)
    row = (lambda b: b) if mod_row is None else (lambda b: mod_row)
    tok = lambda width, cblk: pl.BlockSpec((1, tm, width), lambda b, i: (b, i, cblk))
    gcb = MG_OFF // d
    est = (2 * tm * (F_WIDTH + P_WIDTH + RET_WIDTH) * 2 + 6 * tm * d * 2 + 4 * tm * d * 4
           + (F_WIDTH + P_WIDTH + RET_WIDTH + d) * d * 2 + 4 * tm * d * 4)
    kern = functools.partial(_merge_kernel, final=final)
    return pl.pallas_call(
        kern,
        grid=(bsz, n // tm),
        in_specs=[
            tok(F_WIDTH, 0), tok(P_WIDTH, 0), tok(RET_WIDTH, 0),
            tok(d, gcb), tok(d, gcb + 1), tok(d, gcb + 2),
            tok(d, 0),
            pl.BlockSpec((1, 1, d), lambda b, i: (row(b), 0, 2)),
            _layer_spec(w_up_f.shape, layer), _layer_spec(w_up_p.shape, layer), _layer_spec(w_up_r.shape, layer),
            _layer_spec(w_out.shape, layer),
            _const_spec((1, d)),
        ],
        out_specs=pl.BlockSpec((1, tm, d), lambda b, i: (b, i, 0)),
        out_shape=jax.ShapeDtypeStruct((bsz, n, d), F32),
        compiler_params=_params(("arbitrary", "arbitrary"), est),
        name="merge",
    )(u_f, u_p, u_r, proj, proj, proj, x, mod3, w_up_f, w_up_p, w_up_r, w_out, final_g.reshape(1, d))


def kernel(x, c, ctx, c_ctx, w_ada, b_ada, norm_g, w_in, w_fourier, w_pool, pool_scale, ret_decay_logit,
           w_up_fourier, w_up_pool, w_up_ret, w_out, final_norm_g):
    bsz, n, d = x.shape
    depth = w_ada.shape[0]
    assert d == D_MODEL and bsz + 1 <= MOD_ROWS

    cond_rows = jnp.zeros((MOD_ROWS, d), F32).at[:bsz].set(c).at[bsz].set(c_ctx)
    mod = _ada_call(cond_rows, w_ada, b_ada)

    w_in_b = _wprep_call(w_in)
    w_uf_b, w_upl_b = w_up_fourier.astype(BF16), w_up_pool.astype(BF16)
    w_ur_b, w_out_b = w_up_ret.astype(BF16), w_out.astype(BF16)
    s_zero = jnp.zeros((bsz, RET_HEADS, RET_DK, RET_DK), F32)

    lc = ctx.shape[1]

    def ctx_inproj(l, mod3, col0, ncols):
        p = _inproj_call(ctx.reshape(1, bsz * lc, d), norm_g[l], mod3, bsz, w_in_b, l, col0, ncols)
        return p.reshape(bsz, lc, ncols)

    for l in range(depth):
        last = l == depth - 1
        mod3 = mod[l].reshape(MOD_ROWS, 1, 3 * d)
        if last:
            proj_c = ctx_inproj(l, mod3, R_Q_OFF, MG_OFF - R_Q_OFF)
            s_f, s_b = _retention_call(proj_c, 0, ret_decay_logit[l], s_zero, s_zero,
                                       rope=False, want_out=False, want_states=True)
        else:
            proj_c = ctx_inproj(l, mod3, 0, IN_WIDTH)
            uf_c = _fourier_call(proj_c, w_fourier[l])
            up_c = _pool_call(proj_c, w_pool[l], pool_scale[l])
            ur_c, s_f, s_b = _retention_call(proj_c, R_Q_OFF, ret_decay_logit[l], s_zero, s_zero,
                                             rope=False, want_out=True, want_states=True)
            ctx = _merge_call(uf_c, up_c, ur_c, proj_c, ctx, mod3, bsz, w_uf_b, w_upl_b, w_ur_b,
                              w_out_b, l, final_norm_g, False)
        proj = _inproj_call(x, norm_g[l], mod3, None, w_in_b, l, 0, IN_WIDTH)
        u_f = _fourier_call(proj, w_fourier[l])
        u_p = _pool_call(proj, w_pool[l], pool_scale[l])
        (u_r,) = _retention_call(proj, R_Q_OFF, ret_decay_logit[l], s_f, s_b,
                                 rope=True, want_out=True, want_states=False)
        x = _merge_call(u_f, u_p, u_r, proj, x, mod3, None, w_uf_b, w_upl_b, w_ur_b,
                        w_out_b, l, final_norm_g, last)
    return x
```
